```python
import math
import jax, jax.numpy as jnp
from jax import lax
import numpy as np

D_MODEL = 1024
BATCH = 16
SEQ = 2048
DEPTH = 1

ATTN_HEADS = 8
ATTN_HEAD_DIM = 64
ATTN_V_DIM = 2 * ATTN_HEAD_DIM
ATTN_WIDTH = ATTN_HEADS * ATTN_V_DIM
Q_BLOCK = 128
CHUNK = 128
GMLP_GROUPS = 8
GMLP_GROUP_WIDTH = 128
GMLP_WIDTH = GMLP_GROUPS * GMLP_GROUP_WIDTH
REL_BUCKETS = 32
REL_MAX_EXACT = REL_BUCKETS // 2
REL_MAX_DIST = 128
FFN_HIDDEN = int(math.ceil((8 * D_MODEL / 3) / 256) * 256)
QK_WIDTH = ATTN_HEADS * 2 * ATTN_HEAD_DIM
IN_SPLITS = np.cumsum([QK_WIDTH, QK_WIDTH, ATTN_WIDTH, GMLP_WIDTH, GMLP_WIDTH, D_MODEL]).tolist()
IN_WIDTH = 2 * QK_WIDTH + ATTN_WIDTH + 2 * GMLP_WIDTH + 2 * D_MODEL
RMS_EPS = 1e-6
LN_EPS = 1e-5

kernel_name = "hybrid_diffattn_gmlp_gated_block"


def rmsnorm(x, g, eps=RMS_EPS):
    xf = x.astype(jnp.float32)
    y = xf * lax.rsqrt(jnp.mean(xf * xf, axis=-1, keepdims=True) + eps)
    return (y * g.astype(jnp.float32)).astype(x.dtype)


def layernorm(x, g, b, eps=LN_EPS):
    xf = x.astype(jnp.float32)
    mu = jnp.mean(xf, axis=-1, keepdims=True)
    xc = xf - mu
    y = xc * lax.rsqrt(jnp.mean(xc * xc, axis=-1, keepdims=True) + eps)
    return (y * g.astype(jnp.float32) + b.astype(jnp.float32)).astype(x.dtype)


def rel_bucket(dist):
    n = jnp.maximum(dist, 0)
    is_small = n < REL_MAX_EXACT
    nf = jnp.maximum(n, 1).astype(jnp.float32)
    large = REL_MAX_EXACT + (jnp.log(nf / REL_MAX_EXACT) / math.log(REL_MAX_DIST / REL_MAX_EXACT)
                             * (REL_BUCKETS - REL_MAX_EXACT)).astype(jnp.int32)
    large = jnp.minimum(large, REL_BUCKETS - 1)
    return jnp.where(is_small, n, large)


def diff_attention(q, k, v, lam, rel_bias):
    B, S, H, _, dh = q.shape
    nblk = S // Q_BLOCK
    scale = 1.0 / math.sqrt(dh)
    q_blocks = q.reshape(B, nblk, Q_BLOCK, H, 2, dh).transpose(1, 0, 2, 3, 4, 5)
    k_pos = jnp.arange(S, dtype=jnp.int32)

    def block(args):
        qb, bi = args
        s = jnp.einsum('bqhmd,bkhmd->bhmqk', qb, k,
                       preferred_element_type=jnp.float32) * scale
        q_pos = bi * Q_BLOCK + jnp.arange(Q_BLOCK, dtype=jnp.int32)
        dist = q_pos[:, None] - k_pos[None, :]
        bias = jnp.take(rel_bias.astype(jnp.float32), rel_bucket(dist), axis=0)
        bias = bias.transpose(2, 0, 1)[None, :, None]
        s = jnp.where((dist >= 0)[None, None, None], s + bias, -jnp.inf)
        p = jax.nn.softmax(s, axis=-1)
        a = (p[:, :, 0] - lam * p[:, :, 1]).astype(v.dtype)
        return jnp.einsum('bhqk,bkhe->bqhe', a, v)

    out = lax.map(block, (q_blocks, jnp.arange(nblk, dtype=jnp.int32)))
    return out.transpose(1, 0, 2, 3, 4).reshape(B, S, H, 2 * dh)


def chunked_spatial_gating(u, v, ln_g, ln_b, w_s, b_s):
    B, S, _ = u.shape
    nc = S // CHUNK
    v = layernorm(v, ln_g, ln_b)
    ws = w_s * jnp.tril(jnp.ones((CHUNK, CHUNK), dtype=w_s.dtype))
    vr = v.reshape(B, nc, CHUNK, GMLP_GROUPS, GMLP_GROUP_WIDTH)
    sv = jnp.einsum('gts,bnsgc->bntgc', ws, vr) + b_s.T[None, None, :, :, None]
    return (u.reshape(B, nc, CHUNK, GMLP_GROUPS, GMLP_GROUP_WIDTH) * sv).reshape(B, S, GMLP_WIDTH)


def setup_inputs(seed: int = 0) -> dict:
    key = jax.random.key(seed)
    ks = jax.random.split(key, 24)
    nrm = lambda k, shape, s: jax.random.normal(k, shape, dtype=jnp.float32) * s
    L, D = DEPTH, D_MODEL
    return {
        "x": nrm(ks[0], (BATCH, SEQ, D), 1.0),
        "c": nrm(ks[1], (BATCH, D), 1.0),
        "w_ada": nrm(ks[2], (L, D, 6 * D), D ** -0.5),
        "b_ada": nrm(ks[3], (L, 6 * D), 0.02),
        "norm1_g": 1.0 + nrm(ks[4], (L, D), 0.02),
        "norm2_g": 1.0 + nrm(ks[5], (L, D), 0.02),
        "w_in": nrm(ks[6], (L, D, IN_WIDTH), D ** -0.5),
        "lambda_q1": nrm(ks[7], (L, ATTN_HEAD_DIM), 0.1),
        "lambda_k1": nrm(ks[8], (L, ATTN_HEAD_DIM), 0.1),
        "lambda_q2": nrm(ks[9], (L, ATTN_HEAD_DIM), 0.1),
        "lambda_k2": nrm(ks[10], (L, ATTN_HEAD_DIM), 0.1),
        "subln_g": 1.0 + nrm(ks[11], (L, ATTN_V_DIM), 0.02),
        "ln_v_g": 1.0 + nrm(ks[12], (L, GMLP_WIDTH), 0.02),
        "ln_v_b": nrm(ks[13], (L, GMLP_WIDTH), 0.02),
        "w_spatial": nrm(ks[14], (L, GMLP_GROUPS, CHUNK, CHUNK), CHUNK ** -0.5),
        "b_spatial": 1.0 + nrm(ks[15], (L, GMLP_GROUPS, CHUNK), 0.1),
        "w_proj_a": nrm(ks[16], (L, ATTN_WIDTH, D), ATTN_WIDTH ** -0.5),
        "w_proj_b": nrm(ks[17], (L, GMLP_WIDTH, D), GMLP_WIDTH ** -0.5),
        "w_out": nrm(ks[18], (L, D, D), D ** -0.5),
        "w_ffn_in": nrm(ks[19], (L, D, 2 * FFN_HIDDEN), D ** -0.5),
        "w_ffn_out": nrm(ks[20], (L, FFN_HIDDEN, D), FFN_HIDDEN ** -0.5),
        "rel_bias": nrm(ks[21], (REL_BUCKETS, ATTN_HEADS), 0.5),
        "final_g": 1.0 + nrm(ks[22], (D,), 0.02),
    }


def reference(x, c, w_ada, b_ada, norm1_g, norm2_g, w_in, lambda_q1, lambda_k1, lambda_q2,
              lambda_k2, subln_g, ln_v_g, ln_v_b, w_spatial, b_spatial, w_proj_a, w_proj_b,
              w_out, w_ffn_in, w_ffn_out, rel_bias, final_g):
    B, S, D = x.shape
    c_act = jax.nn.silu(c)
    for l in range(DEPTH):
        mod = c_act @ w_ada[l] + b_ada[l]
        sh1, sc1, g1, sh2, sc2, g2 = [m[:, None, :] for m in jnp.split(mod, 6, axis=-1)]

        h = rmsnorm(x, norm1_g[l]) * (1.0 + sc1) + sh1
        z = h @ w_in[l]
        zq, zk, zv, zu, zg, ga, gb = jnp.split(z, IN_SPLITS, axis=-1)

        lam_init = 0.8 - 0.6 * math.exp(-0.3 * l)
        lam = (jnp.exp(jnp.sum(lambda_q1[l] * lambda_k1[l]).astype(jnp.float32))
               - jnp.exp(jnp.sum(lambda_q2[l] * lambda_k2[l]).astype(jnp.float32)) + lam_init)
        q = zq.reshape(B, S, ATTN_HEADS, 2, ATTN_HEAD_DIM)
        k = zk.reshape(B, S, ATTN_HEADS, 2, ATTN_HEAD_DIM)
        v = zv.reshape(B, S, ATTN_HEADS, ATTN_V_DIM)
        oa = diff_attention(q, k, v, lam, rel_bias)
        oa = (rmsnorm(oa, subln_g[l]) * (1.0 - lam_init)).reshape(B, S, ATTN_WIDTH)

        ob = chunked_spatial_gating(jax.nn.gelu(zu), jax.nn.gelu(zg), ln_v_g[l], ln_v_b[l],
                                    w_spatial[l], b_spatial[l])

        merged = jax.nn.sigmoid(ga) * (oa @ w_proj_a[l]) + jax.nn.sigmoid(gb) * (ob @ w_proj_b[l])
        x = x + g1 * (merged @ w_out[l])

        h2 = rmsnorm(x, norm2_g[l]) * (1.0 + sc2) + sh2
        gate, up = jnp.split(h2 @ w_ffn_in[l], 2, axis=-1)
        x = x + g2 * ((jax.nn.silu(gate) * up) @ w_ffn_out[l])

    return rmsnorm(x, final_g)
```

```python
import functools
import math

import numpy as np
import jax
import jax.numpy as jnp
from jax import lax
from jax.experimental import pallas as pl
from jax.experimental.pallas import tpu as pltpu

F32 = jnp.float32
BF16 = jnp.bfloat16

RMS_EPS = 1e-6
LN_EPS = 1e-5
HEAD_DIM = 64
V_DIM = 2 * HEAD_DIM
CHUNK = 128
GROUP_WIDTH = 128
REL_BUCKETS = 32
REL_MAX_EXACT = REL_BUCKETS // 2
REL_MAX_DIST = 128
LAM_INIT = 0.8 - 0.6 * math.exp(-0.3 * 0)
MASK_VALUE = -1e30

SEQ_TILE = 256
TAIL_TILE = 256
V7X_VMEM_BYTES = 64 * 1024 * 1024

_NT = (((1,), (1,)), ((), ()))


def _sigmoid(x):
    return 1.0 / (1.0 + jnp.exp(-x))


def _gelu_tanh(x):
    c = math.sqrt(2.0 / math.pi)
    return 0.5 * x * (1.0 + jnp.tanh(c * (x + 0.044715 * (x * x * x))))


def _resident(shape):
    n = len(shape)
    return pl.BlockSpec(shape, lambda *_: (0,) * n, pipeline_mode=pl.Buffered(1))


def _ada_kernel(c_ref, w_ref, b_ref, o_ref):
    c = c_ref[...]
    ca = c * _sigmoid(c)
    o_ref[...] = jnp.dot(ca, w_ref[...], precision=lax.Precision.HIGHEST,
                         preferred_element_type=F32) + b_ref[...]


def _ada(c, w_ada, b_ada):
    B, D = c.shape
    N = w_ada.shape[1]
    tn = 1024
    return pl.pallas_call(
        _ada_kernel,
        grid=(N // tn,),
        in_specs=[pl.BlockSpec((B, D), lambda n: (0, 0)),
                  pl.BlockSpec((D, tn), lambda n: (0, n)),
                  pl.BlockSpec((1, tn), lambda n: (0, n))],
        out_specs=pl.BlockSpec((B, tn), lambda n: (0, n)),
        out_shape=jax.ShapeDtypeStruct((B, N), F32),
        name="ada",
    )(c, w_ada, b_ada.reshape(1, N))


def _bucket_tiles(T):
    kk = np.arange(T, dtype=np.int32)[:, None]
    qq = np.arange(T, dtype=np.int32)[None, :]
    out = []
    for o in range(2):
        dist = o * T + qq - kk
        n = np.maximum(dist, 0)
        nf = np.maximum(n, 1).astype(np.float32)
        large = REL_MAX_EXACT + (np.log(nf / np.float32(REL_MAX_EXACT))
                                 / np.float32(math.log(REL_MAX_DIST / REL_MAX_EXACT))
                                 * np.float32(REL_BUCKETS - REL_MAX_EXACT)).astype(np.int32)
        large = np.minimum(large, REL_BUCKETS - 1)
        b = np.where(n < REL_MAX_EXACT, n, large)
        out.append(np.where(dist < 0, -1, b).astype(np.int32))
    return np.stack(out)


def _bias_kernel(rb_ref, bucket_ref, o_ref):
    h = pl.program_id(0)
    bk = bucket_ref[...]
    far = rb_ref[REL_BUCKETS - 1, h]
    acc = jnp.where(bk < 0, MASK_VALUE, 0.0).astype(F32)
    for b in range(REL_BUCKETS - 1):
        acc = jnp.where(bk == b, rb_ref[b, h] - far, acc)
    o_ref[...] = acc


def _bias_tiles(rel_bias, T):
    H = rel_bias.shape[1]
    buckets = jnp.asarray(_bucket_tiles(T))
    return pl.pallas_call(
        _bias_kernel,
        grid=(H,),
        in_specs=[pl.BlockSpec(memory_space=pltpu.SMEM),
                  pl.BlockSpec((2, T, T), lambda h: (0, 0, 0))],
        out_specs=pl.BlockSpec((None, 2, T, T), lambda h: (h, 0, 0, 0)),
        out_shape=jax.ShapeDtypeStruct((H, 2, T, T), F32),
        name="bias_tiles",
    )(rel_bias, buckets)


def _inproj_kernel(x_ref, mod_ref, n1g_ref, wqT_ref, wk_ref, wvT_ref, wu_ref, wg_ref, wga_ref,
                   wgb_ref, lng_ref, lnb_ref, ws_ref, bs_ref,
                   qT_ref, k_ref, vT_ref, ob_ref, sga_ref, sgb_ref):
    tm = x_ref.shape[0]
    x = x_ref[...]
    y = x * lax.rsqrt(jnp.mean(x * x, axis=-1, keepdims=True) + RMS_EPS) * n1g_ref[...]
    h = (y * (1.0 + mod_ref[1:2, :]) + mod_ref[0:1, :]).astype(BF16)

    qT = lax.dot_general(wqT_ref[...], h, _NT, preferred_element_type=F32)
    qT_ref[...] = (qT * (1.0 / math.sqrt(HEAD_DIM))).astype(BF16)
    k_ref[...] = jnp.dot(h, wk_ref[...], preferred_element_type=F32).astype(BF16)
    vT_ref[...] = lax.dot_general(wvT_ref[...], h, _NT, preferred_element_type=F32).astype(BF16)

    sga_ref[...] = _sigmoid(jnp.dot(h, wga_ref[...], preferred_element_type=F32)).astype(BF16)
    sgb_ref[...] = _sigmoid(jnp.dot(h, wgb_ref[...], preferred_element_type=F32)).astype(BF16)

    u = _gelu_tanh(jnp.dot(h, wu_ref[...], preferred_element_type=F32))
    g = _gelu_tanh(jnp.dot(h, wg_ref[...], preferred_element_type=F32))
    mu = jnp.mean(g, axis=-1, keepdims=True)
    gc = g - mu
    v = gc * lax.rsqrt(jnp.mean(gc * gc, axis=-1, keepdims=True) + LN_EPS)
    v = (v * lng_ref[...] + lnb_ref[...]).astype(BF16)
    tri = (lax.broadcasted_iota(jnp.int32, (CHUNK, CHUNK), 0)
           >= lax.broadcasted_iota(jnp.int32, (CHUNK, CHUNK), 1))
    n_groups = ws_ref.shape[0]
    for gi in range(n_groups):
        ws = jnp.where(tri, ws_ref[gi], 0.0).astype(BF16)
        bcol = bs_ref[:, gi:gi + 1]
        cols = slice(gi * GROUP_WIDTH, (gi + 1) * GROUP_WIDTH)
        for ci in range(tm // CHUNK):
            rows = slice(ci * CHUNK, (ci + 1) * CHUNK)
            sv = jnp.dot(ws, v[rows, cols], preferred_element_type=F32) + bcol
            ob_ref[rows, cols] = (u[rows, cols] * sv).astype(BF16)


def _inproj(x, mod3, norm1_g, w_in, ln_v_g, ln_v_b, w_spatial, b_spatial):
    B, S, D = x.shape
    T = SEQ_TILE
    nt = S // T
    wq, wk, wv, wu, wg, wga, wgb = [w_in[:, i * D:(i + 1) * D].astype(BF16) for i in range(7)]
    wqT = wq.T
    wvT = wv.T
    G = w_spatial.shape[0]
    row = lambda a: a.reshape(1, D)
    tok = pl.BlockSpec((None, T, D), lambda b, t: (b, t, 0))
    featT = pl.BlockSpec((None, None, D, T), lambda b, t: (b, t, 0, 0))
    out_tok = jax.ShapeDtypeStruct((B, S, D), BF16)
    out_featT = jax.ShapeDtypeStruct((B, nt, D, T), BF16)
    return pl.pallas_call(
        _inproj_kernel,
        grid=(B, nt),
        in_specs=[tok,
                  pl.BlockSpec((None, 6, D), lambda b, t: (b, 0, 0)),
                  _resident((1, D)),
                  _resident((D, D)), _resident((D, D)), _resident((D, D)), _resident((D, D)),
                  _resident((D, D)), _resident((D, D)), _resident((D, D)),
                  _resident((1, D)), _resident((1, D)),
                  _resident((G, CHUNK, CHUNK)), _resident((CHUNK, G))],
        out_specs=[featT, tok, featT, tok, tok, tok],
        out_shape=[out_featT, out_tok, out_featT, out_tok, out_tok, out_tok],
        compiler_params=pltpu.CompilerParams(
            dimension_semantics=("arbitrary", "arbitrary"),
            vmem_limit_bytes=48 * 1024 * 1024),
        name="inproj",
    )(x, mod3, row(norm1_g), wqT, wk, wvT, wu, wg, wga, wgb, row(ln_v_g), row(ln_v_b),
      w_spatial, b_spatial.T)


def _attn_kernel(qT_ref, k_ref, vT_ref, tbl_ref, lam_ref, sg_ref, o_ref, acc_ref, m_ref, l_ref):
    nt, _, T = qT_ref.shape
    feat = lax.broadcasted_iota(jnp.int32, (V_DIM, T), 0)

    def q_body(i, carry):
        qf = qT_ref[i].astype(F32)
        qz = (jnp.where(feat < HEAD_DIM, qf, 0.0).astype(BF16),
              jnp.where(feat >= HEAD_DIM, qf, 0.0).astype(BF16))
        m_ref[...] = jnp.full(m_ref.shape, MASK_VALUE, F32)
        l_ref[...] = jnp.zeros(l_ref.shape, F32)
        acc_ref[...] = jnp.zeros(acc_ref.shape, F32)

        def tile(j, tbl):
            kt = k_ref[pl.ds(pl.multiple_of(j * T, T), T), :]
            vt = vT_ref[j]
            for mi in range(2):
                s = jnp.dot(kt, qz[mi], preferred_element_type=F32)
                if tbl is not None:
                    s = s + tbl
                m_old = m_ref[mi]
                m_new = jnp.maximum(m_old, jnp.max(s, axis=0, keepdims=True))
                a = jnp.exp(m_old - m_new)
                p = jnp.exp(s - m_new)
                l_ref[mi] = a * l_ref[mi] + jnp.sum(p, axis=0, keepdims=True)
                acc_ref[mi] = a * acc_ref[mi] + jnp.dot(vt, p.astype(BF16),
                                                        preferred_element_type=F32)
                m_ref[mi] = m_new

        def far_body(j, c):
            tile(j, None)
            return c

        lax.fori_loop(0, jnp.maximum(i - 1, 0), far_body, 0)

        @pl.when(i >= 1)
        def _():
            tile(i - 1, tbl_ref[1])

        tile(i, tbl_ref[0])

        lam = (jnp.exp(jnp.sum(lam_ref[0:1, :] * lam_ref[1:2, :], axis=-1, keepdims=True))
               - jnp.exp(jnp.sum(lam_ref[2:3, :] * lam_ref[3:4, :], axis=-1, keepdims=True))
               + LAM_INIT)
        oT = acc_ref[0] / l_ref[0] - lam * (acc_ref[1] / l_ref[1])
        yT = oT * lax.rsqrt(jnp.mean(oT * oT, axis=0, keepdims=True) + RMS_EPS)
        y = yT.T * sg_ref[...] * (1.0 - LAM_INIT)
        o_ref[pl.ds(pl.multiple_of(i * T, T), T), :] = y.astype(BF16)
        return carry

    lax.fori_loop(0, nt, q_body, 0)


def _attention(qT, k, vT, tbl, lam_vecs, subln_g):
    B, nt, D, T = qT.shape
    S = nt * T
    H = D // V_DIM
    return pl.pallas_call(
        _attn_kernel,
        grid=(B, H),
        in_specs=[pl.BlockSpec((None, nt, V_DIM, T), lambda b, h: (b, 0, h, 0)),
                  pl.BlockSpec((None, S, V_DIM), lambda b, h: (b, 0, h)),
                  pl.BlockSpec((None, nt, V_DIM, T), lambda b, h: (b, 0, h, 0)),
                  pl.BlockSpec((None, 2, T, T), lambda b, h: (h, 0, 0, 0)),
                  pl.BlockSpec((4, HEAD_DIM), lambda b, h: (0, 0)),
                  pl.BlockSpec((1, V_DIM), lambda b, h: (0, 0))],
        out_specs=pl.BlockSpec((None, S, V_DIM), lambda b, h: (b, 0, h)),
        out_shape=jax.ShapeDtypeStruct((B, S, D), BF16),
        scratch_shapes=[pltpu.VMEM((2, V_DIM, T), F32),
                        pltpu.VMEM((2, 1, T), F32),
                        pltpu.VMEM((2, 1, T), F32)],
        compiler_params=pltpu.CompilerParams(
            dimension_semantics=("arbitrary", "arbitrary"),
            vmem_limit_bytes=32 * 1024 * 1024),
        name="attn",
    )(qT, k, vT, tbl, lam_vecs, subln_g.reshape(1, V_DIM))


def _tail_kernel(x_ref, oa_ref, ob_ref, sga_ref, sgb_ref, mod_ref, n2g_ref, fg_ref,
                 wpa_ref, wpb_ref, wo_ref, wfi_ref, wfo_ref, o_ref):
    hidden = wfo_ref.shape[0]
    pa = jnp.dot(oa_ref[...], wpa_ref[...], preferred_element_type=F32)
    pb = jnp.dot(ob_ref[...], wpb_ref[...], preferred_element_type=F32)
    merged = (sga_ref[...].astype(F32) * pa + sgb_ref[...].astype(F32) * pb).astype(BF16)
    x1 = x_ref[...] + mod_ref[2:3, :] * jnp.dot(merged, wo_ref[...], preferred_element_type=F32)

    y = x1 * lax.rsqrt(jnp.mean(x1 * x1, axis=-1, keepdims=True) + RMS_EPS) * n2g_ref[...]
    h2 = (y * (1.0 + mod_ref[4:5, :]) + mod_ref[3:4, :]).astype(BF16)
    gu = jnp.dot(h2, wfi_ref[...], preferred_element_type=F32)
    gate = gu[:, :hidden]
    up = gu[:, hidden:]
    act = (gate * _sigmoid(gate) * up).astype(BF16)
    x2 = x1 + mod_ref[5:6, :] * jnp.dot(act, wfo_ref[...], preferred_element_type=F32)

    o_ref[...] = x2 * lax.rsqrt(jnp.mean(x2 * x2, axis=-1, keepdims=True) + RMS_EPS) * fg_ref[...]


def _tail(x, oa, ob, sga, sgb, mod3, norm2_g, final_g, w_proj_a, w_proj_b, w_out, w_ffn_in,
          w_ffn_out):
    B, S, D = x.shape
    T = TAIL_TILE
    hidden = w_ffn_out.shape[0]
    row = lambda a: a.reshape(1, D)
    tok = pl.BlockSpec((None, T, D), lambda b, t: (b, t, 0))
    return pl.pallas_call(
        _tail_kernel,
        grid=(B, S // T),
        in_specs=[tok, tok, tok, tok, tok,
                  pl.BlockSpec((None, 6, D), lambda b, t: (b, 0, 0)),
                  _resident((1, D)), _resident((1, D)),
                  _resident((D, D)), _resident((D, D)), _resident((D, D)),
                  _resident((D, 2 * hidden)), _resident((hidden, D))],
        out_specs=tok,
        out_shape=jax.ShapeDtypeStruct((B, S, D), F32),
        compiler_params=pltpu.CompilerParams(
            dimension_semantics=("arbitrary", "arbitrary"),
            vmem_limit_bytes=56 * 1024 * 1024),
        name="tail",
    )(x, oa, ob, sga, sgb, mod3, row(norm2_g), row(final_g),
      w_proj_a.astype(BF16), w_proj_b.astype(BF16), w_out.astype(BF16),
      w_ffn_in.astype(BF16), w_ffn_out.astype(BF16))


def kernel(x, c, w_ada, b_ada, norm1_g, norm2_g, w_in, lambda_q1, lambda_k1, lambda_q2, lambda_k2,
           subln_g, ln_v_g, ln_v_b, w_spatial, b_spatial, w_proj_a, w_proj_b, w_out, w_ffn_in,
           w_ffn_out, rel_bias, final_g):
    B, S, D = x.shape
    assert w_ada.shape[0] == 1, "single-layer block"
    assert S % SEQ_TILE == 0 and S % TAIL_TILE == 0 and SEQ_TILE % CHUNK == 0
    assert SEQ_TILE >= REL_MAX_DIST, "bias must be constant beyond the first off-diagonal tile"

    mod3 = _ada(c, w_ada[0], b_ada[0]).reshape(B, 6, D)
    tbl = _bias_tiles(rel_bias, SEQ_TILE)
    qT, k, vT, ob, sga, sgb = _inproj(x, mod3, norm1_g[0], w_in[0], ln_v_g[0], ln_v_b[0],
                                      w_spatial[0], b_spatial[0])
    lam_vecs = jnp.concatenate([lambda_q1, lambda_k1, lambda_q2, lambda_k2], axis=0)
    oa = _attention(qT, k, vT, tbl, lam_vecs, subln_g[0])
    return _tail(x, oa, ob, sga, sgb, mod3, norm2_g[0], final_g, w_proj_a[0], w_proj_b[0],
                 w_out[0], w_ffn_in[0], w_ffn_out[0])
```

```python
import functools
import math

import numpy as np
import jax
import jax.numpy as jnp
from jax import lax
from jax.experimental import pallas as pl
from jax.experimental.pallas import tpu as pltpu

F32 = jnp.float32
BF16 = jnp.bfloat16

RMS_EPS = 1e-6
LN_EPS = 1e-5
HEAD_DIM = 64
V_DIM = 2 * HEAD_DIM
CHUNK = 128
GROUP_WIDTH = 128
REL_BUCKETS = 32
REL_MAX_EXACT = REL_BUCKETS // 2
REL_MAX_DIST = 128
LAM_INIT = 0.8 - 0.6 * math.exp(-0.3 * 0)
MASK_VALUE = -1e30

SEQ_TILE = 256
TAIL_TILE = 256
V7X_VMEM_BYTES = 64 * 1024 * 1024

_NT = (((1,), (1,)), ((), ()))


def _sigmoid(x):
    return 1.0 / (1.0 + jnp.exp(-x))


def _gelu_tanh(x):
    c = math.sqrt(2.0 / math.pi)
    return 0.5 * x * (1.0 + jnp.tanh(c * (x + 0.044715 * (x * x * x))))


def _resident(shape):
    n = len(shape)
    return pl.BlockSpec(shape, lambda *_: (0,) * n, pipeline_mode=pl.Buffered(1))


def _ada_kernel(c_ref, w_ref, b_ref, o_ref):
    c = c_ref[...]
    ca = c * _sigmoid(c)
    o_ref[...] = jnp.dot(ca, w_ref[...], precision=lax.Precision.HIGHEST,
                         preferred_element_type=F32) + b_ref[...]


def _ada(c, w_ada, b_ada):
    B, D = c.shape
    N = w_ada.shape[1]
    tn = 1024
    return pl.pallas_call(
        _ada_kernel,
        grid=(N // tn,),
        in_specs=[pl.BlockSpec((B, D), lambda n: (0, 0)),
                  pl.BlockSpec((D, tn), lambda n: (0, n)),
                  pl.BlockSpec((1, tn), lambda n: (0, n))],
        out_specs=pl.BlockSpec((B, tn), lambda n: (0, n)),
        out_shape=jax.ShapeDtypeStruct((B, N), F32),
        name="ada",
    )(c, w_ada, b_ada.reshape(1, N))


def _bucket_tiles(T):
    kk = np.arange(T, dtype=np.int32)[:, None]
    qq = np.arange(T, dtype=np.int32)[None, :]
    out = []
    for o in (1, 0):
        dist = o * T + qq - kk
        n = np.maximum(dist, 0)
        nf = np.maximum(n, 1).astype(np.float32)
        large = REL_MAX_EXACT + (np.log(nf / np.float32(REL_MAX_EXACT))
                                 / np.float32(math.log(REL_MAX_DIST / REL_MAX_EXACT))
                                 * np.float32(REL_BUCKETS - REL_MAX_EXACT)).astype(np.int32)
        large = np.minimum(large, REL_BUCKETS - 1)
        b = np.where(n < REL_MAX_EXACT, n, large)
        out.append(np.where(dist < 0, -1, b).astype(np.int32))
    return np.concatenate(out, axis=0)


def _bias_kernel(rb_ref, bucket_ref, o_ref):
    h = pl.program_id(0)
    bk = bucket_ref[...]
    far = rb_ref[REL_BUCKETS - 1, h]
    acc = jnp.where(bk < 0, MASK_VALUE, 0.0).astype(F32)
    for b in range(REL_BUCKETS - 1):
        acc = jnp.where(bk == b, rb_ref[b, h] - far, acc)
    o_ref[...] = acc


def _bias_tiles(rel_bias, T):
    H = rel_bias.shape[1]
    buckets = jnp.asarray(_bucket_tiles(T))
    return pl.pallas_call(
        _bias_kernel,
        grid=(H,),
        in_specs=[pl.BlockSpec(memory_space=pltpu.SMEM),
                  pl.BlockSpec((2 * T, T), lambda h: (0, 0))],
        out_specs=pl.BlockSpec((None, 2 * T, T), lambda h: (h, 0, 0)),
        out_shape=jax.ShapeDtypeStruct((H, 2 * T, T), F32),
        name="bias_tiles",
    )(rel_bias, buckets)


def _inproj_kernel(x_ref, mod_ref, n1g_ref, wqT_ref, wk_ref, wvT_ref, wu_ref, wg_ref, wga_ref,
                   wgb_ref, lng_ref, lnb_ref, ws_ref, bs_ref,
                   qT_ref, k_ref, vT_ref, ob_ref, sga_ref, sgb_ref):
    tm = x_ref.shape[0]
    x = x_ref[...]
    y = x * lax.rsqrt(jnp.mean(x * x, axis=-1, keepdims=True) + RMS_EPS) * n1g_ref[...]
    h = (y * (1.0 + mod_ref[1:2, :]) + mod_ref[0:1, :]).astype(BF16)

    qT = lax.dot_general(wqT_ref[...], h, _NT, preferred_element_type=F32)
    qT_ref[...] = (qT * (1.0 / math.sqrt(HEAD_DIM))).astype(BF16)
    k_ref[...] = jnp.dot(h, wk_ref[...], preferred_element_type=F32).astype(BF16)
    vT_ref[...] = lax.dot_general(wvT_ref[...], h, _NT, preferred_element_type=F32).astype(BF16)

    sga_ref[...] = _sigmoid(jnp.dot(h, wga_ref[...], preferred_element_type=F32)).astype(BF16)
    sgb_ref[...] = _sigmoid(jnp.dot(h, wgb_ref[...], preferred_element_type=F32)).astype(BF16)

    u = _gelu_tanh(jnp.dot(h, wu_ref[...], preferred_element_type=F32))
    g = _gelu_tanh(jnp.dot(h, wg_ref[...], preferred_element_type=F32))
    mu = jnp.mean(g, axis=-1, keepdims=True)
    gc = g - mu
    v = gc * lax.rsqrt(jnp.mean(gc * gc, axis=-1, keepdims=True) + LN_EPS)
    v = (v * lng_ref[...] + lnb_ref[...]).astype(BF16)
    tri = (lax.broadcasted_iota(jnp.int32, (CHUNK, CHUNK), 0)
           >= lax.broadcasted_iota(jnp.int32, (CHUNK, CHUNK), 1))
    n_groups = ws_ref.shape[0]
    for gi in range(n_groups):
        ws = jnp.where(tri, ws_ref[gi], 0.0).astype(BF16)
        bcol = bs_ref[:, gi:gi + 1]
        cols = slice(gi * GROUP_WIDTH, (gi + 1) * GROUP_WIDTH)
        for ci in range(tm // CHUNK):
            rows = slice(ci * CHUNK, (ci + 1) * CHUNK)
            sv = jnp.dot(ws, v[rows, cols], preferred_element_type=F32) + bcol
            ob_ref[rows, cols] = (u[rows, cols] * sv).astype(BF16)


def _inproj(x, mod3, norm1_g, w_in, ln_v_g, ln_v_b, w_spatial, b_spatial):
    B, S, D = x.shape
    T = SEQ_TILE
    nt = S // T
    wq, wk, wv, wu, wg, wga, wgb = [w_in[:, i * D:(i + 1) * D].astype(BF16) for i in range(7)]
    wqT = wq.T
    wvT = wv.T
    G = w_spatial.shape[0]
    row = lambda a: a.reshape(1, D)
    tok = pl.BlockSpec((None, T, D), lambda b, t: (b, t, 0))
    q_featT = pl.BlockSpec((None, None, D, T), lambda b, t: (b, t, 0, 0))
    v_featT = pl.BlockSpec((None, D, T), lambda b, t: (b, 0, t))
    out_tok = jax.ShapeDtypeStruct((B, S, D), BF16)
    return pl.pallas_call(
        _inproj_kernel,
        grid=(B, nt),
        in_specs=[tok,
                  pl.BlockSpec((None, 6, D), lambda b, t: (b, 0, 0)),
                  _resident((1, D)),
                  _resident((D, D)), _resident((D, D)), _resident((D, D)), _resident((D, D)),
                  _resident((D, D)), _resident((D, D)), _resident((D, D)),
                  _resident((1, D)), _resident((1, D)),
                  _resident((G, CHUNK, CHUNK)), _resident((CHUNK, G))],
        out_specs=[q_featT, tok, v_featT, tok, tok, tok],
        out_shape=[jax.ShapeDtypeStruct((B, nt, D, T), BF16), out_tok,
                   jax.ShapeDtypeStruct((B, D, S), BF16), out_tok, out_tok, out_tok],
        compiler_params=pltpu.CompilerParams(
            dimension_semantics=("arbitrary", "arbitrary"),
            vmem_limit_bytes=48 * 1024 * 1024),
        name="inproj",
    )(x, mod3, row(norm1_g), wqT, wk, wvT, wu, wg, wga, wgb, row(ln_v_g), row(ln_v_b),
      w_spatial, b_spatial.T)


def _attn_kernel(qT_ref, k_ref, vT_ref, tbl_ref, lam_ref, sg_ref, o_ref):
    nt, _, T = qT_ref.shape
    feat = lax.broadcasted_iota(jnp.int32, (V_DIM, T), 0)
    lam = (jnp.exp(jnp.sum(lam_ref[0:1, :] * lam_ref[1:2, :], axis=-1, keepdims=True))
           - jnp.exp(jnp.sum(lam_ref[2:3, :] * lam_ref[3:4, :], axis=-1, keepdims=True))
           + LAM_INIT)

    for i in range(nt):
        qf = qT_ref[i].astype(F32)
        near_lo = max(i - 1, 0) * T
        hi = (i + 1) * T
        tbl = tbl_ref[...] if i >= 1 else tbl_ref[T:, :]
        outs = []
        for in_map in (feat < HEAD_DIM, feat >= HEAD_DIM):
            qz = jnp.where(in_map, qf, 0.0).astype(BF16)
            s_near = jnp.dot(k_ref[near_lo:hi, :], qz, preferred_element_type=F32) + tbl
            m = jnp.max(s_near, axis=0, keepdims=True)
            if near_lo > 0:
                s_far = jnp.dot(k_ref[0:near_lo, :], qz, preferred_element_type=F32)
                m = jnp.maximum(m, jnp.max(s_far, axis=0, keepdims=True))
            p_near = jnp.exp(s_near - m)
            l = jnp.sum(p_near, axis=0, keepdims=True)
            acc = jnp.dot(vT_ref[:, near_lo:hi], p_near.astype(BF16), preferred_element_type=F32)
            if near_lo > 0:
                p_far = jnp.exp(s_far - m)
                l = l + jnp.sum(p_far, axis=0, keepdims=True)
                acc = acc + jnp.dot(vT_ref[:, 0:near_lo], p_far.astype(BF16),
                                    preferred_element_type=F32)
            outs.append(acc * (1.0 / l))
        oT = outs[0] - lam * outs[1]
        yT = oT * lax.rsqrt(jnp.mean(oT * oT, axis=0, keepdims=True) + RMS_EPS)
        y = yT.T * sg_ref[...] * (1.0 - LAM_INIT)
        o_ref[i * T:(i + 1) * T, :] = y.astype(BF16)


def _attention(qT, k, vT, tbl, lam_vecs, subln_g):
    B, nt, D, T = qT.shape
    S = nt * T
    H = D // V_DIM
    return pl.pallas_call(
        _attn_kernel,
        grid=(B, H),
        in_specs=[pl.BlockSpec((None, nt, V_DIM, T), lambda b, h: (b, 0, h, 0)),
                  pl.BlockSpec((None, S, V_DIM), lambda b, h: (b, 0, h)),
                  pl.BlockSpec((None, V_DIM, S), lambda b, h: (b, h, 0)),
                  pl.BlockSpec((None, 2 * T, T), lambda b, h: (h, 0, 0)),
                  pl.BlockSpec((4, HEAD_DIM), lambda b, h: (0, 0)),
                  pl.BlockSpec((1, V_DIM), lambda b, h: (0, 0))],
        out_specs=pl.BlockSpec((None, S, V_DIM), lambda b, h: (b, 0, h)),
        out_shape=jax.ShapeDtypeStruct((B, S, D), BF16),
        compiler_params=pltpu.CompilerParams(
            dimension_semantics=("arbitrary", "arbitrary"),
            vmem_limit_bytes=48 * 1024 * 1024),
        name="attn",
    )(qT, k, vT, tbl, lam_vecs, subln_g.reshape(1, V_DIM))


def _tail_kernel(x_ref, oa_ref, ob_ref, sga_ref, sgb_ref, mod_ref, n2g_ref, fg_ref,
                 wpa_ref, wpb_ref, wo_ref, wfi_ref, wfo_ref, o_ref):
    hidden = wfo_ref.shape[0]
    pa = jnp.dot(oa_ref[...], wpa_ref[...], preferred_element_type=F32)
    pb = jnp.dot(ob_ref[...], wpb_ref[...], preferred_element_type=F32)
    merged = (sga_ref[...].astype(F32) * pa + sgb_ref[...].astype(F32) * pb).astype(BF16)
    x1 = x_ref[...] + mod_ref[2:3, :] * jnp.dot(merged, wo_ref[...], preferred_element_type=F32)

    y = x1 * lax.rsqrt(jnp.mean(x1 * x1, axis=-1, keepdims=True) + RMS_EPS) * n2g_ref[...]
    h2 = (y * (1.0 + mod_ref[4:5, :]) + mod_ref[3:4, :]).astype(BF16)
    gu = jnp.dot(h2, wfi_ref[...], preferred_element_type=F32)
    gate = gu[:, :hidden]
    up = gu[:, hidden:]
    act = (gate * _sigmoid(gate) * up).astype(BF16)
    x2 = x1 + mod_ref[5:6, :] * jnp.dot(act, wfo_ref[...], preferred_element_type=F32)

    o_ref[...] = x2 * lax.rsqrt(jnp.mean(x2 * x2, axis=-1, keepdims=True) + RMS_EPS) * fg_ref[...]


def _tail(x, oa, ob, sga, sgb, mod3, norm2_g, final_g, w_proj_a, w_proj_b, w_out, w_ffn_in,
          w_ffn_out):
    B, S, D = x.shape
    T = TAIL_TILE
    hidden = w_ffn_out.shape[0]
    row = lambda a: a.reshape(1, D)
    tok = pl.BlockSpec((None, T, D), lambda b, t: (b, t, 0))
    return pl.pallas_call(
        _tail_kernel,
        grid=(B, S // T),
        in_specs=[tok, tok, tok, tok, tok,
                  pl.BlockSpec((None, 6, D), lambda b, t: (b, 0, 0)),
                  _resident((1, D)), _resident((1, D)),
                  _resident((D, D)), _resident((D, D)), _resident((D, D)),
                  _resident((D, 2 * hidden)), _resident((hidden, D))],
        out_specs=tok,
        out_shape=jax.ShapeDtypeStruct((B, S, D), F32),
        compiler_params=pltpu.CompilerParams(
            dimension_semantics=("arbitrary", "arbitrary"),
            vmem_limit_bytes=56 * 1024 * 1024),
        name="tail",
    )(x, oa, ob, sga, sgb, mod3, row(norm2_g), row(final_g),
      w_proj_a.astype(BF16), w_proj_b.astype(BF16), w_out.astype(BF16),
      w_ffn_in.astype(BF16), w_ffn_out.astype(BF16))


def kernel(x, c, w_ada, b_ada, norm1_g, norm2_g, w_in, lambda_q1, lambda_k1, lambda_q2, lambda_k2,
           subln_g, ln_v_g, ln_v_b, w_spatial, b_spatial, w_proj_a, w_proj_b, w_out, w_ffn_in,
           w_ffn_out, rel_bias, final_g):
    B, S, D = x.shape
    assert w_ada.shape[0] == 1, "single-layer block"
    assert S % SEQ_TILE == 0 and S % TAIL_TILE == 0 and SEQ_TILE % CHUNK == 0
    assert SEQ_TILE >= REL_MAX_DIST, "bias must be constant beyond the first off-diagonal tile"

    mod3 = _ada(c, w_ada[0], b_ada[0]).reshape(B, 6, D)
    tbl = _bias_tiles(rel_bias, SEQ_TILE)
    qT, k, vT, ob, sga, sgb = _inproj(x, mod3, norm1_g[0], w_in[0], ln_v_g[0], ln_v_b[0],
                                      w_spatial[0], b_spatial[0])
    lam_vecs = jnp.concatenate([lambda_q1, lambda_k1, lambda_q2, lambda_k2], axis=0)
    oa = _attention(qT, k, vT, tbl, lam_vecs, subln_g[0])
    return _tail(x, oa, ob, sga, sgb, mod3, norm2_g[0], final_g, w_proj_a[0], w_proj_b[0],
                 w_out[0], w_ffn_in[0], w_ffn_out[0])
```

```python
import functools
import math

import numpy as np
import jax
import jax.numpy as jnp
from jax import lax
from jax.experimental import pallas as pl
from jax.experimental.pallas import tpu as pltpu

F32 = jnp.float32
BF16 = jnp.bfloat16

RMS_EPS = 1e-6
LN_EPS = 1e-5
HEAD_DIM = 64
V_DIM = 2 * HEAD_DIM
CHUNK = 128
GROUP_WIDTH = 128
REL_BUCKETS = 32
REL_MAX_EXACT = REL_BUCKETS // 2
REL_MAX_DIST = 128
LAM_INIT = 0.8 - 0.6 * math.exp(-0.3 * 0)
MASK_VALUE = -1e30
LOG2E = math.log2(math.e)
ONES_ROWS = 16

SEQ_TILE = 256
TAIL_TILE = 256
V7X_VMEM_BYTES = 64 * 1024 * 1024

_NT = (((1,), (1,)), ((), ()))


def _sigmoid(x):
    return 1.0 / (1.0 + jnp.exp(-x))


def _gelu_tanh(x):
    c = math.sqrt(2.0 / math.pi)
    return 0.5 * x * (1.0 + jnp.tanh(c * (x + 0.044715 * (x * x * x))))


def _resident(shape):
    n = len(shape)
    return pl.BlockSpec(shape, lambda *_: (0,) * n, pipeline_mode=pl.Buffered(1))


def _ada_kernel(c_ref, w_ref, b_ref, o_ref):
    c = c_ref[...]
    ca = c * _sigmoid(c)
    o_ref[...] = jnp.dot(ca, w_ref[...], precision=lax.Precision.HIGHEST,
                         preferred_element_type=F32) + b_ref[...]


def _ada(c, w_ada, b_ada):
    B, D = c.shape
    N = w_ada.shape[1]
    tn = 1024
    return pl.pallas_call(
        _ada_kernel,
        grid=(N // tn,),
        in_specs=[pl.BlockSpec((B, D), lambda n: (0, 0)),
                  pl.BlockSpec((D, tn), lambda n: (0, n)),
                  pl.BlockSpec((1, tn), lambda n: (0, n))],
        out_specs=pl.BlockSpec((B, tn), lambda n: (0, n)),
        out_shape=jax.ShapeDtypeStruct((B, N), F32),
        name="ada",
    )(c, w_ada, b_ada.reshape(1, N))


def _bucket_tiles(T):
    kk = np.arange(T, dtype=np.int32)[:, None]
    qq = np.arange(T, dtype=np.int32)[None, :]
    out = []
    for o in (1, 0):
        dist = o * T + qq - kk
        n = np.maximum(dist, 0)
        nf = np.maximum(n, 1).astype(np.float32)
        large = REL_MAX_EXACT + (np.log(nf / np.float32(REL_MAX_EXACT))
                                 / np.float32(math.log(REL_MAX_DIST / REL_MAX_EXACT))
                                 * np.float32(REL_BUCKETS - REL_MAX_EXACT)).astype(np.int32)
        large = np.minimum(large, REL_BUCKETS - 1)
        b = np.where(n < REL_MAX_EXACT, n, large)
        out.append(np.where(dist < 0, -1, b).astype(np.int32))
    return np.concatenate(out, axis=0)


def _bias_kernel(rb_ref, bucket_ref, o_ref):
    h = pl.program_id(0)
    bk = bucket_ref[...]
    far = rb_ref[REL_BUCKETS - 1, h]
    acc = jnp.where(bk < 0, MASK_VALUE, 0.0).astype(F32)
    for b in range(REL_BUCKETS - 1):
        acc = jnp.where(bk == b, (rb_ref[b, h] - far) * LOG2E, acc)
    o_ref[...] = acc


def _bias_tiles(rel_bias, T):
    H = rel_bias.shape[1]
    buckets = jnp.asarray(_bucket_tiles(T))
    return pl.pallas_call(
        _bias_kernel,
        grid=(H,),
        in_specs=[pl.BlockSpec(memory_space=pltpu.SMEM),
                  pl.BlockSpec((2 * T, T), lambda h: (0, 0))],
        out_specs=pl.BlockSpec((None, 2 * T, T), lambda h: (h, 0, 0)),
        out_shape=jax.ShapeDtypeStruct((H, 2 * T, T), F32),
        name="bias_tiles",
    )(rel_bias, buckets)


def _inproj_kernel(x_ref, mod_ref, n1g_ref, wqT_ref, wk_ref, wvT_ref, wu_ref, wg_ref, wga_ref,
                   wgb_ref, lng_ref, lnb_ref, ws_ref, bs_ref,
                   qT_ref, k_ref, vT_ref, ob_ref, sga_ref, sgb_ref):
    tm = x_ref.shape[0]
    x = x_ref[...]
    y = x * lax.rsqrt(jnp.mean(x * x, axis=-1, keepdims=True) + RMS_EPS) * n1g_ref[...]
    h = (y * (1.0 + mod_ref[1:2, :]) + mod_ref[0:1, :]).astype(BF16)

    qT = lax.dot_general(wqT_ref[...], h, _NT, preferred_element_type=F32)
    qT_ref[...] = (qT * (LOG2E / math.sqrt(HEAD_DIM))).astype(BF16)
    k_ref[...] = jnp.dot(h, wk_ref[...], preferred_element_type=F32).astype(BF16)
    vT_ref[...] = lax.dot_general(wvT_ref[...], h, _NT, preferred_element_type=F32).astype(BF16)

    sga_ref[...] = _sigmoid(jnp.dot(h, wga_ref[...], preferred_element_type=F32)).astype(BF16)
    sgb_ref[...] = _sigmoid(jnp.dot(h, wgb_ref[...], preferred_element_type=F32)).astype(BF16)

    u = _gelu_tanh(jnp.dot(h, wu_ref[...], preferred_element_type=F32))
    g = _gelu_tanh(jnp.dot(h, wg_ref[...], preferred_element_type=F32))
    mu = jnp.mean(g, axis=-1, keepdims=True)
    gc = g - mu
    v = gc * lax.rsqrt(jnp.mean(gc * gc, axis=-1, keepdims=True) + LN_EPS)
    v = (v * lng_ref[...] + lnb_ref[...]).astype(BF16)
    tri = (lax.broadcasted_iota(jnp.int32, (CHUNK, CHUNK), 0)
           >= lax.broadcasted_iota(jnp.int32, (CHUNK, CHUNK), 1))
    n_groups = ws_ref.shape[0]
    for gi in range(n_groups):
        ws = jnp.where(tri, ws_ref[gi], 0.0).astype(BF16)
        bcol = bs_ref[:, gi:gi + 1]
        cols = slice(gi * GROUP_WIDTH, (gi + 1) * GROUP_WIDTH)
        for ci in range(tm // CHUNK):
            rows = slice(ci * CHUNK, (ci + 1) * CHUNK)
            sv = jnp.dot(ws, v[rows, cols], preferred_element_type=F32) + bcol
            ob_ref[rows, cols] = (u[rows, cols] * sv).astype(BF16)


def _inproj(x, mod3, norm1_g, w_in, ln_v_g, ln_v_b, w_spatial, b_spatial):
    B, S, D = x.shape
    T = SEQ_TILE
    nt = S // T
    wq, wk, wv, wu, wg, wga, wgb = [w_in[:, i * D:(i + 1) * D].astype(BF16) for i in range(7)]
    wqT = wq.T
    wvT = wv.T
    G = w_spatial.shape[0]
    row = lambda a: a.reshape(1, D)
    tok = pl.BlockSpec((None, T, D), lambda b, t: (b, t, 0))
    q_featT = pl.BlockSpec((None, None, D, T), lambda b, t: (b, t, 0, 0))
    v_featT = pl.BlockSpec((None, D, T), lambda b, t: (b, 0, t))
    out_tok = jax.ShapeDtypeStruct((B, S, D), BF16)
    return pl.pallas_call(
        _inproj_kernel,
        grid=(B, nt),
        in_specs=[tok,
                  pl.BlockSpec((None, 6, D), lambda b, t: (b, 0, 0)),
                  _resident((1, D)),
                  _resident((D, D)), _resident((D, D)), _resident((D, D)), _resident((D, D)),
                  _resident((D, D)), _resident((D, D)), _resident((D, D)),
                  _resident((1, D)), _resident((1, D)),
                  _resident((G, CHUNK, CHUNK)), _resident((CHUNK, G))],
        out_specs=[q_featT, tok, v_featT, tok, tok, tok],
        out_shape=[jax.ShapeDtypeStruct((B, nt, D, T), BF16), out_tok,
                   jax.ShapeDtypeStruct((B, D, S), BF16), out_tok, out_tok, out_tok],
        compiler_params=pltpu.CompilerParams(
            dimension_semantics=("arbitrary", "arbitrary"),
            vmem_limit_bytes=48 * 1024 * 1024),
        name="inproj",
    )(x, mod3, row(norm1_g), wqT, wk, wvT, wu, wg, wga, wgb, row(ln_v_g), row(ln_v_b),
      w_spatial, b_spatial.T)


def _attn_kernel(qT_ref, k_ref, vT_ref, tbl_ref, lam_ref, sg_ref, o_ref, va_ref):
    nt, _, T = qT_ref.shape
    feat = lax.broadcasted_iota(jnp.int32, (V_DIM, T), 0)
    va_ref[0:V_DIM, :] = vT_ref[...]
    va_ref[V_DIM:, :] = jnp.ones((ONES_ROWS, va_ref.shape[1]), BF16)
    lam = (jnp.exp(jnp.sum(lam_ref[0:1, :] * lam_ref[1:2, :], axis=-1, keepdims=True))
           - jnp.exp(jnp.sum(lam_ref[2:3, :] * lam_ref[3:4, :], axis=-1, keepdims=True))
           + LAM_INIT)

    units = [(i, mi) for i in range(nt) for mi in range(2)]
    state = {}

    def scores(u):
        i, mi = u
        in_map = (feat < HEAD_DIM) if mi == 0 else (feat >= HEAD_DIM)
        qz = jnp.where(in_map, qT_ref[i].astype(F32), 0.0).astype(BF16)
        s_list, m = [], None
        for c in range(i + 1):
            s = jnp.dot(k_ref[c * T:(c + 1) * T, :], qz, preferred_element_type=F32)
            if c == i:
                s = s + tbl_ref[T:, :]
            elif c == i - 1:
                s = s + tbl_ref[:T, :]
            mc = jnp.max(s, axis=0, keepdims=True)
            m = mc if m is None else jnp.maximum(m, mc)
            s_list.append(s)
            yield
        state[u] = {"s": s_list, "m": m}

    def probs(u):
        st = state[u]
        p_list = []
        for s in st.pop("s"):
            p_list.append(jnp.exp2(s - st["m"]).astype(BF16))
            yield
        st["p"] = p_list

    def values(u):
        i, mi = u
        acc = None
        for c, p in enumerate(state[u].pop("p")):
            d = jnp.dot(va_ref[:, c * T:(c + 1) * T], p, preferred_element_type=F32)
            acc = d if acc is None else acc + d
            yield
        state[u] = acc[0:V_DIM, :] * (1.0 / acc[V_DIM:V_DIM + 1, :])
        if mi == 1:
            oT = state.pop((i, 0)) - lam * state.pop((i, 1))
            yT = oT * lax.rsqrt(jnp.mean(oT * oT, axis=0, keepdims=True) + RMS_EPS)
            y = yT.T * sg_ref[...] * (1.0 - LAM_INIT)
            o_ref[i * T:(i + 1) * T, :] = y.astype(BF16)

    n = len(units)
    for t in range(n + 2):
        live = []
        if t < n:
            live.append(scores(units[t]))
        if 1 <= t <= n:
            live.append(probs(units[t - 1]))
        if 2 <= t:
            live.append(values(units[t - 2]))
        while live:
            live = [g for g in live if next(g, StopIteration) is not StopIteration]


def _attention(qT, k, vT, tbl, lam_vecs, subln_g):
    B, nt, D, T = qT.shape
    S = nt * T
    H = D // V_DIM
    return pl.pallas_call(
        _attn_kernel,
        grid=(B, H),
        in_specs=[pl.BlockSpec((None, nt, V_DIM, T), lambda b, h: (b, 0, h, 0)),
                  pl.BlockSpec((None, S, V_DIM), lambda b, h: (b, 0, h)),
                  pl.BlockSpec((None, V_DIM, S), lambda b, h: (b, h, 0)),
                  pl.BlockSpec((None, 2 * T, T), lambda b, h: (h, 0, 0)),
                  pl.BlockSpec((4, HEAD_DIM), lambda b, h: (0, 0)),
                  pl.BlockSpec((1, V_DIM), lambda b, h: (0, 0))],
        out_specs=pl.BlockSpec((None, S, V_DIM), lambda b, h: (b, 0, h)),
        out_shape=jax.ShapeDtypeStruct((B, S, D), BF16),
        scratch_shapes=[pltpu.VMEM((V_DIM + ONES_ROWS, S), BF16)],
        compiler_params=pltpu.CompilerParams(
            dimension_semantics=("arbitrary", "arbitrary"),
            vmem_limit_bytes=48 * 1024 * 1024),
        name="attn",
    )(qT, k, vT, tbl, lam_vecs, subln_g.reshape(1, V_DIM))


def _tail_kernel(x_ref, oa_ref, ob_ref, sga_ref, sgb_ref, mod_ref, n2g_ref, fg_ref,
                 wpa_ref, wpb_ref, wo_ref, wfi_ref, wfo_ref, o_ref):
    hidden = wfo_ref.shape[0]
    pa = jnp.dot(oa_ref[...], wpa_ref[...], preferred_element_type=F32)
    pb = jnp.dot(ob_ref[...], wpb_ref[...], preferred_element_type=F32)
    merged = (sga_ref[...].astype(F32) * pa + sgb_ref[...].astype(F32) * pb).astype(BF16)
    x1 = x_ref[...] + mod_ref[2:3, :] * jnp.dot(merged, wo_ref[...], preferred_element_type=F32)

    y = x1 * lax.rsqrt(jnp.mean(x1 * x1, axis=-1, keepdims=True) + RMS_EPS) * n2g_ref[...]
    h2 = (y * (1.0 + mod_ref[4:5, :]) + mod_ref[3:4, :]).astype(BF16)
    gu = jnp.dot(h2, wfi_ref[...], preferred_element_type=F32)
    gate = gu[:, :hidden]
    up = gu[:, hidden:]
    act = (gate * _sigmoid(gate) * up).astype(BF16)
    x2 = x1 + mod_ref[5:6, :] * jnp.dot(act, wfo_ref[...], preferred_element_type=F32)

    o_ref[...] = x2 * lax.rsqrt(jnp.mean(x2 * x2, axis=-1, keepdims=True) + RMS_EPS) * fg_ref[...]


def _tail(x, oa, ob, sga, sgb, mod3, norm2_g, final_g, w_proj_a, w_proj_b, w_out, w_ffn_in,
          w_ffn_out):
    B, S, D = x.shape
    T = TAIL_TILE
    hidden = w_ffn_out.shape[0]
    row = lambda a: a.reshape(1, D)
    tok = pl.BlockSpec((None, T, D), lambda b, t: (b, t, 0))
    return pl.pallas_call(
        _tail_kernel,
        grid=(B, S // T),
        in_specs=[tok, tok, tok, tok, tok,
                  pl.BlockSpec((None, 6, D), lambda b, t: (b, 0, 0)),
                  _resident((1, D)), _resident((1, D)),
                  _resident((D, D)), _resident((D, D)), _resident((D, D)),
                  _resident((D, 2 * hidden)), _resident((hidden, D))],
        out_specs=tok,
        out_shape=jax.ShapeDtypeStruct((B, S, D), F32),
        compiler_params=pltpu.CompilerParams(
            dimension_semantics=("arbitrary", "arbitrary"),
            vmem_limit_bytes=56 * 1024 * 1024),
        name="tail",
    )(x, oa, ob, sga, sgb, mod3, row(norm2_g), row(final_g),
      w_proj_a.astype(BF16), w_proj_b.astype(BF16), w_out.astype(BF16),
      w_ffn_in.astype(BF16), w_ffn_out.astype(BF16))


def kernel(x, c, w_ada, b_ada, norm1_g, norm2_g, w_in, lambda_q1, lambda_k1, lambda_q2, lambda_k2,
           subln_g, ln_v_g, ln_v_b, w_spatial, b_spatial, w_proj_a, w_proj_b, w_out, w_ffn_in,
           w_ffn_out, rel_bias, final_g):
    B, S, D = x.shape
    assert w_ada.shape[0] == 1, "single-layer block"
    assert S % SEQ_TILE == 0 and S % TAIL_TILE == 0 and SEQ_TILE % CHUNK == 0
    assert SEQ_TILE >= REL_MAX_DIST, "bias must be constant beyond the first off-diagonal tile"

    mod3 = _ada(c, w_ada[0], b_ada[0]).reshape(B, 6, D)
    tbl = _bias_tiles(rel_bias, SEQ_TILE)
    qT, k, vT, ob, sga, sgb = _inproj(x, mod3, norm1_g[0], w_in[0], ln_v_g[0], ln_v_b[0],
                                      w_spatial[0], b_spatial[0])
    lam_vecs = jnp.concatenate([lambda_q1, lambda_k1, lambda_q2, lambda_k2], axis=0)
    oa = _attention(qT, k, vT, tbl, lam_vecs, subln_g[0])
    return _tail(x, oa, ob, sga, sgb, mod3, norm2_g[0], final_g, w_proj_a[0], w_proj_b[0],
                 w_out[0], w_ffn_in[0], w_ffn_out[0])
```

```python
import functools
import math

import numpy as np
import jax
import jax.numpy as jnp
from jax import lax
from jax.experimental import pallas as pl
from jax.experimental.pallas import tpu as pltpu

F32 = jnp.float32
BF16 = jnp.bfloat16

RMS_EPS = 1e-6
LN_EPS = 1e-5
HEAD_DIM = 64
V_DIM = 2 * HEAD_DIM
CHUNK = 128
GROUP_WIDTH = 128
REL_BUCKETS = 32
REL_MAX_EXACT = REL_BUCKETS // 2
REL_MAX_DIST = 128
LAM_INIT = 0.8 - 0.6 * math.exp(-0.3 * 0)
MASK_VALUE = -1e30
LOG2E = math.log2(math.e)
ONES_ROWS = 16
STAGE_LAG = 2

SEQ_TILE = 256
TAIL_TILE = 256
V7X_VMEM_BYTES = 64 * 1024 * 1024

_NT = (((1,), (1,)), ((), ()))


def _sigmoid(x):
    return 1.0 / (1.0 + jnp.exp(-x))


def _gelu_tanh(x):
    c = math.sqrt(2.0 / math.pi)
    return 0.5 * x * (1.0 + jnp.tanh(c * (x + 0.044715 * (x * x * x))))


def _resident(shape):
    n = len(shape)
    return pl.BlockSpec(shape, lambda *_: (0,) * n, pipeline_mode=pl.Buffered(1))


def _ada_kernel(c_ref, w_ref, b_ref, o_ref):
    c = c_ref[...]
    ca = c * _sigmoid(c)
    o_ref[...] = jnp.dot(ca, w_ref[...], precision=lax.Precision.HIGHEST,
                         preferred_element_type=F32) + b_ref[...]


def _ada(c, w_ada, b_ada):
    B, D = c.shape
    N = w_ada.shape[1]
    tn = 1024
    return pl.pallas_call(
        _ada_kernel,
        grid=(N // tn,),
        in_specs=[pl.BlockSpec((B, D), lambda n: (0, 0)),
                  pl.BlockSpec((D, tn), lambda n: (0, n)),
                  pl.BlockSpec((1, tn), lambda n: (0, n))],
        out_specs=pl.BlockSpec((B, tn), lambda n: (0, n)),
        out_shape=jax.ShapeDtypeStruct((B, N), F32),
        name="ada",
    )(c, w_ada, b_ada.reshape(1, N))


def _bucket_tiles(T):
    kk = np.arange(T, dtype=np.int32)[:, None]
    qq = np.arange(T, dtype=np.int32)[None, :]
    out = []
    for o in (1, 0):
        dist = o * T + qq - kk
        n = np.maximum(dist, 0)
        nf = np.maximum(n, 1).astype(np.float32)
        large = REL_MAX_EXACT + (np.log(nf / np.float32(REL_MAX_EXACT))
                                 / np.float32(math.log(REL_MAX_DIST / REL_MAX_EXACT))
                                 * np.float32(REL_BUCKETS - REL_MAX_EXACT)).astype(np.int32)
        large = np.minimum(large, REL_BUCKETS - 1)
        b = np.where(n < REL_MAX_EXACT, n, large)
        out.append(np.where(dist < 0, -1, b).astype(np.int32))
    return np.concatenate(out, axis=0)


def _bias_kernel(rb_ref, bucket_ref, o_ref):
    h = pl.program_id(0)
    bk = bucket_ref[...]
    far = rb_ref[REL_BUCKETS - 1, h]
    acc = jnp.where(bk < 0, MASK_VALUE, 0.0).astype(F32)
    for b in range(REL_BUCKETS - 1):
        acc = jnp.where(bk == b, (rb_ref[b, h] - far) * LOG2E, acc)
    o_ref[...] = acc


def _bias_tiles(rel_bias, T):
    H = rel_bias.shape[1]
    buckets = jnp.asarray(_bucket_tiles(T))
    return pl.pallas_call(
        _bias_kernel,
        grid=(H,),
        in_specs=[pl.BlockSpec(memory_space=pltpu.SMEM),
                  pl.BlockSpec((2 * T, T), lambda h: (0, 0))],
        out_specs=pl.BlockSpec((None, 2 * T, T), lambda h: (h, 0, 0)),
        out_shape=jax.ShapeDtypeStruct((H, 2 * T, T), F32),
        name="bias_tiles",
    )(rel_bias, buckets)


def _inproj_kernel(x_ref, mod_ref, n1g_ref, wqT_ref, wk_ref, wvT_ref, wu_ref, wg_ref, wga_ref,
                   wgb_ref, lng_ref, lnb_ref, ws_ref, bs_ref,
                   qT_ref, k_ref, vT_ref, ob_ref, sga_ref, sgb_ref):
    tm = x_ref.shape[0]
    x = x_ref[...]
    y = x * lax.rsqrt(jnp.mean(x * x, axis=-1, keepdims=True) + RMS_EPS) * n1g_ref[...]
    h = (y * (1.0 + mod_ref[1:2, :]) + mod_ref[0:1, :]).astype(BF16)

    qT = lax.dot_general(wqT_ref[...], h, _NT, preferred_element_type=F32)
    qT_ref[...] = (qT * (LOG2E / math.sqrt(HEAD_DIM))).astype(BF16)
    k_ref[...] = jnp.dot(h, wk_ref[...], preferred_element_type=F32).astype(BF16)
    vT_ref[...] = lax.dot_general(wvT_ref[...], h, _NT, preferred_element_type=F32).astype(BF16)

    sga_ref[...] = _sigmoid(jnp.dot(h, wga_ref[...], preferred_element_type=F32)).astype(BF16)
    sgb_ref[...] = _sigmoid(jnp.dot(h, wgb_ref[...], preferred_element_type=F32)).astype(BF16)

    u = _gelu_tanh(jnp.dot(h, wu_ref[...], preferred_element_type=F32))
    g = _gelu_tanh(jnp.dot(h, wg_ref[...], preferred_element_type=F32))
    mu = jnp.mean(g, axis=-1, keepdims=True)
    gc = g - mu
    v = gc * lax.rsqrt(jnp.mean(gc * gc, axis=-1, keepdims=True) + LN_EPS)
    v = (v * lng_ref[...] + lnb_ref[...]).astype(BF16)
    tri = (lax.broadcasted_iota(jnp.int32, (CHUNK, CHUNK), 0)
           >= lax.broadcasted_iota(jnp.int32, (CHUNK, CHUNK), 1))
    n_groups = ws_ref.shape[0]
    for gi in range(n_groups):
        ws = jnp.where(tri, ws_ref[gi], 0.0).astype(BF16)
        bcol = bs_ref[:, gi:gi + 1]
        cols = slice(gi * GROUP_WIDTH, (gi + 1) * GROUP_WIDTH)
        for ci in range(tm // CHUNK):
            rows = slice(ci * CHUNK, (ci + 1) * CHUNK)
            sv = jnp.dot(ws, v[rows, cols], preferred_element_type=F32) + bcol
            ob_ref[rows, cols] = (u[rows, cols] * sv).astype(BF16)


def _inproj(x, mod3, norm1_g, w_in, ln_v_g, ln_v_b, w_spatial, b_spatial):
    B, S, D = x.shape
    T = SEQ_TILE
    nt = S // T
    wq, wk, wv, wu, wg, wga, wgb = [w_in[:, i * D:(i + 1) * D].astype(BF16) for i in range(7)]
    wqT = wq.T
    wvT = wv.T
    G = w_spatial.shape[0]
    row = lambda a: a.reshape(1, D)
    tok = pl.BlockSpec((None, T, D), lambda b, t: (b, t, 0))
    q_featT = pl.BlockSpec((None, None, D, T), lambda b, t: (b, t, 0, 0))
    v_featT = pl.BlockSpec((None, D, T), lambda b, t: (b, 0, t))
    out_tok = jax.ShapeDtypeStruct((B, S, D), BF16)
    return pl.pallas_call(
        _inproj_kernel,
        grid=(B, nt),
        in_specs=[tok,
                  pl.BlockSpec((None, 6, D), lambda b, t: (b, 0, 0)),
                  _resident((1, D)),
                  _resident((D, D)), _resident((D, D)), _resident((D, D)), _resident((D, D)),
                  _resident((D, D)), _resident((D, D)), _resident((D, D)),
                  _resident((1, D)), _resident((1, D)),
                  _resident((G, CHUNK, CHUNK)), _resident((CHUNK, G))],
        out_specs=[q_featT, tok, v_featT, tok, tok, tok],
        out_shape=[jax.ShapeDtypeStruct((B, nt, D, T), BF16), out_tok,
                   jax.ShapeDtypeStruct((B, D, S), BF16), out_tok, out_tok, out_tok],
        compiler_params=pltpu.CompilerParams(
            dimension_semantics=("arbitrary", "arbitrary"),
            vmem_limit_bytes=48 * 1024 * 1024),
        name="inproj",
    )(x, mod3, row(norm1_g), wqT, wk, wvT, wu, wg, wga, wgb, row(ln_v_g), row(ln_v_b),
      w_spatial, b_spatial.T)


def _attn_kernel(qT_ref, k_ref, vT_ref, tbl_ref, lam_ref, sg_ref, o_ref, va_ref):
    nt, _, T = qT_ref.shape
    feat = lax.broadcasted_iota(jnp.int32, (V_DIM, T), 0)
    va_ref[0:V_DIM, :] = vT_ref[...]
    va_ref[V_DIM:, :] = jnp.ones((ONES_ROWS, va_ref.shape[1]), BF16)
    lam = (jnp.exp(jnp.sum(lam_ref[0:1, :] * lam_ref[1:2, :], axis=-1, keepdims=True))
           - jnp.exp(jnp.sum(lam_ref[2:3, :] * lam_ref[3:4, :], axis=-1, keepdims=True))
           + LAM_INIT)

    units = [(i, mi) for i in reversed(range(nt)) for mi in range(2)]
    tick = [0]
    s_done = {}
    p_done = {}
    outs = {}

    def scores():
        for u in units:
            i, mi = u
            in_map = (feat < HEAD_DIM) if mi == 0 else (feat >= HEAD_DIM)
            qz = jnp.where(in_map, qT_ref[i].astype(F32), 0.0).astype(BF16)
            s_list, m = [], None
            for c in range(i + 1):
                s = jnp.dot(k_ref[c * T:(c + 1) * T, :], qz, preferred_element_type=F32)
                if c == i:
                    s = s + tbl_ref[T:, :]
                elif c == i - 1:
                    s = s + tbl_ref[:T, :]
                mc = jnp.max(s, axis=0, keepdims=True)
                m = mc if m is None else jnp.maximum(m, mc)
                s_list.append(s)
                yield True
            s_done[u] = (s_list, m, tick[0])

    def probs():
        for u in units:
            while u not in s_done or tick[0] < s_done[u][2] + STAGE_LAG:
                yield False
            s_list, m, _ = s_done.pop(u)
            for c, s in enumerate(s_list):
                p_done[u, c] = (jnp.exp2((s - m).astype(BF16)), tick[0])
                yield True

    def values():
        for u in units:
            i, mi = u
            acc = None
            for c in range(i + 1):
                while (u, c) not in p_done or tick[0] < p_done[u, c][1] + STAGE_LAG:
                    yield False
                p, _ = p_done.pop((u, c))
                d = jnp.dot(va_ref[:, c * T:(c + 1) * T], p, preferred_element_type=F32)
                acc = d if acc is None else acc + d
                yield True
            outs[u] = acc[0:V_DIM, :] * (1.0 / acc[V_DIM:V_DIM + 1, :])
            if mi == 1:
                oT = outs.pop((i, 0)) - lam * outs.pop((i, 1))
                yT = oT * lax.rsqrt(jnp.mean(oT * oT, axis=0, keepdims=True) + RMS_EPS)
                y = yT.T * sg_ref[...] * (1.0 - LAM_INIT)
                o_ref[i * T:(i + 1) * T, :] = y.astype(BF16)

    streams = [scores(), probs(), values()]
    while streams:
        streams = [g for g in streams if next(g, None) is not None]
        tick[0] += 1


def _attention(qT, k, vT, tbl, lam_vecs, subln_g):
    B, nt, D, T = qT.shape
    S = nt * T
    H = D // V_DIM
    return pl.pallas_call(
        _attn_kernel,
        grid=(B, H),
        in_specs=[pl.BlockSpec((None, nt, V_DIM, T), lambda b, h: (b, 0, h, 0)),
                  pl.BlockSpec((None, S, V_DIM), lambda b, h: (b, 0, h)),
                  pl.BlockSpec((None, V_DIM, S), lambda b, h: (b, h, 0)),
                  pl.BlockSpec((None, 2 * T, T), lambda b, h: (h, 0, 0)),
                  pl.BlockSpec((4, HEAD_DIM), lambda b, h: (0, 0)),
                  pl.BlockSpec((1, V_DIM), lambda b, h: (0, 0))],
        out_specs=pl.BlockSpec((None, S, V_DIM), lambda b, h: (b, 0, h)),
        out_shape=jax.ShapeDtypeStruct((B, S, D), BF16),
        scratch_shapes=[pltpu.VMEM((V_DIM + ONES_ROWS, S), BF16)],
        compiler_params=pltpu.CompilerParams(
            dimension_semantics=("arbitrary", "arbitrary"),
            vmem_limit_bytes=48 * 1024 * 1024),
        name="attn",
    )(qT, k, vT, tbl, lam_vecs, subln_g.reshape(1, V_DIM))


def _tail_kernel(x_ref, oa_ref, ob_ref, sga_ref, sgb_ref, mod_ref, n2g_ref, fg_ref,
                 wpa_ref, wpb_ref, wo_ref, wfi_ref, wfo_ref, o_ref):
    hidden = wfo_ref.shape[0]
    pa = jnp.dot(oa_ref[...], wpa_ref[...], preferred_element_type=F32)
    pb = jnp.dot(ob_ref[...], wpb_ref[...], preferred_element_type=F32)
    merged = (sga_ref[...].astype(F32) * pa + sgb_ref[...].astype(F32) * pb).astype(BF16)
    x1 = x_ref[...] + mod_ref[2:3, :] * jnp.dot(merged, wo_ref[...], preferred_element_type=F32)

    y = x1 * lax.rsqrt(jnp.mean(x1 * x1, axis=-1, keepdims=True) + RMS_EPS) * n2g_ref[...]
    h2 = (y * (1.0 + mod_ref[4:5, :]) + mod_ref[3:4, :]).astype(BF16)
    gu = jnp.dot(h2, wfi_ref[...], preferred_element_type=F32)
    gate = gu[:, :hidden]
    up = gu[:, hidden:]
    act = (gate * _sigmoid(gate) * up).astype(BF16)
    x2 = x1 + mod_ref[5:6, :] * jnp.dot(act, wfo_ref[...], preferred_element_type=F32)

    o_ref[...] = x2 * lax.rsqrt(jnp.mean(x2 * x2, axis=-1, keepdims=True) + RMS_EPS) * fg_ref[...]


def _tail(x, oa, ob, sga, sgb, mod3, norm2_g, final_g, w_proj_a, w_proj_b, w_out, w_ffn_in,
          w_ffn_out):
    B, S, D = x.shape
    T = TAIL_TILE
    hidden = w_ffn_out.shape[0]
    row = lambda a: a.reshape(1, D)
    tok = pl.BlockSpec((None, T, D), lambda b, t: (b, t, 0))
    return pl.pallas_call(
        _tail_kernel,
        grid=(B, S // T),
        in_specs=[tok, tok, tok, tok, tok,
                  pl.BlockSpec((None, 6, D), lambda b, t: (b, 0, 0)),
                  _resident((1, D)), _resident((1, D)),
                  _resident((D, D)), _resident((D, D)), _resident((D, D)),
                  _resident((D, 2 * hidden)), _resident((hidden, D))],
        out_specs=tok,
        out_shape=jax.ShapeDtypeStruct((B, S, D), F32),
        compiler_params=pltpu.CompilerParams(
            dimension_semantics=("arbitrary", "arbitrary"),
            vmem_limit_bytes=56 * 1024 * 1024),
        name="tail",
    )(x, oa, ob, sga, sgb, mod3, row(norm2_g), row(final_g),
      w_proj_a.astype(BF16), w_proj_b.astype(BF16), w_out.astype(BF16),
      w_ffn_in.astype(BF16), w_ffn_out.astype(BF16))


def kernel(x, c, w_ada, b_ada, norm1_g, norm2_g, w_in, lambda_q1, lambda_k1, lambda_q2, lambda_k2,
           subln_g, ln_v_g, ln_v_b, w_spatial, b_spatial, w_proj_a, w_proj_b, w_out, w_ffn_in,
           w_ffn_out, rel_bias, final_g):
    B, S, D = x.shape
    assert w_ada.shape[0] == 1, "single-layer block"
    assert S % SEQ_TILE == 0 and S % TAIL_TILE == 0 and SEQ_TILE % CHUNK == 0
    assert SEQ_TILE >= REL_MAX_DIST, "bias must be constant beyond the first off-diagonal tile"

    mod3 = _ada(c, w_ada[0], b_ada[0]).reshape(B, 6, D)
    tbl = _bias_tiles(rel_bias, SEQ_TILE)
    qT, k, vT, ob, sga, sgb = _inproj(x, mod3, norm1_g[0], w_in[0], ln_v_g[0], ln_v_b[0],
                                      w_spatial[0], b_spatial[0])
    lam_vecs = jnp.concatenate([lambda_q1, lambda_k1, lambda_q2, lambda_k2], axis=0)
    oa = _attention(qT, k, vT, tbl, lam_vecs, subln_g[0])
    return _tail(x, oa, ob, sga, sgb, mod3, norm2_g[0], final_g, w_proj_a[0], w_proj_b[0],
                 w_out[0], w_ffn_in[0], w_ffn_out[0])
```

```python
import functools
import math

import numpy as np
import jax
import jax.numpy as jnp
from jax import lax
from jax.experimental import pallas as pl
from jax.experimental.pallas import tpu as pltpu

F32 = jnp.float32
BF16 = jnp.bfloat16

RMS_EPS = 1e-6
LN_EPS = 1e-5
HEAD_DIM = 64
V_DIM = 2 * HEAD_DIM
CHUNK = 128
GROUP_WIDTH = 128
REL_BUCKETS = 32
REL_MAX_EXACT = REL_BUCKETS // 2
REL_MAX_DIST = 128
LAM_INIT = 0.8 - 0.6 * math.exp(-0.3 * 0)
MASK_VALUE = -1e30
LOG2E = math.log2(math.e)
ONES_ROWS = 16
STAGE_LAG = 2

SEQ_TILE = 256
INPROJ_TILE = 512
INPROJ_COLS = 256
TAIL_TILE = 512
TAIL_ROWS = 256
TAIL_HIDDEN_CHUNK = 1536
V7X_VMEM_BYTES = 64 * 1024 * 1024

_NT = (((1,), (1,)), ((), ()))


def _sigmoid(x):
    return 1.0 / (1.0 + jnp.exp(-x))


def _gelu_tanh(x):
    c = math.sqrt(2.0 / math.pi)
    return 0.5 * x * (1.0 + jnp.tanh(c * (x + 0.044715 * (x * x * x))))


def _resident(shape):
    n = len(shape)
    return pl.BlockSpec(shape, lambda *_: (0,) * n, pipeline_mode=pl.Buffered(1))


def _ada_kernel(c_ref, w_ref, b_ref, o_ref):
    c = c_ref[...]
    ca = c * _sigmoid(c)
    o_ref[...] = jnp.dot(ca, w_ref[...], precision=lax.Precision.HIGHEST,
                         preferred_element_type=F32) + b_ref[...]


def _ada(c, w_ada, b_ada):
    B, D = c.shape
    N = w_ada.shape[1]
    tn = 1024
    return pl.pallas_call(
        _ada_kernel,
        grid=(N // tn,),
        in_specs=[pl.BlockSpec((B, D), lambda n: (0, 0)),
                  pl.BlockSpec((D, tn), lambda n: (0, n)),
                  pl.BlockSpec((1, tn), lambda n: (0, n))],
        out_specs=pl.BlockSpec((B, tn), lambda n: (0, n)),
        out_shape=jax.ShapeDtypeStruct((B, N), F32),
        name="ada",
    )(c, w_ada, b_ada.reshape(1, N))


def _bucket_tiles(T):
    kk = np.arange(T, dtype=np.int32)[:, None]
    qq = np.arange(T, dtype=np.int32)[None, :]
    out = []
    for o in (1, 0):
        dist = o * T + qq - kk
        n = np.maximum(dist, 0)
        nf = np.maximum(n, 1).astype(np.float32)
        large = REL_MAX_EXACT + (np.log(nf / np.float32(REL_MAX_EXACT))
                                 / np.float32(math.log(REL_MAX_DIST / REL_MAX_EXACT))
                                 * np.float32(REL_BUCKETS - REL_MAX_EXACT)).astype(np.int32)
        large = np.minimum(large, REL_BUCKETS - 1)
        b = np.where(n < REL_MAX_EXACT, n, large)
        out.append(np.where(dist < 0, -1, b).astype(np.int32))
    return np.concatenate(out, axis=0)


def _bias_kernel(rb_ref, bucket_ref, o_ref):
    h = pl.program_id(0)
    bk = bucket_ref[...]
    far = rb_ref[REL_BUCKETS - 1, h]
    acc = jnp.where(bk < 0, MASK_VALUE, 0.0).astype(F32)
    for b in range(REL_BUCKETS - 1):
        acc = jnp.where(bk == b, (rb_ref[b, h] - far) * LOG2E, acc)
    o_ref[...] = acc


def _bias_tiles(rel_bias, T):
    H = rel_bias.shape[1]
    buckets = jnp.asarray(_bucket_tiles(T))
    return pl.pallas_call(
        _bias_kernel,
        grid=(H,),
        in_specs=[pl.BlockSpec(memory_space=pltpu.SMEM),
                  pl.BlockSpec((2 * T, T), lambda h: (0, 0))],
        out_specs=pl.BlockSpec((None, 2 * T, T), lambda h: (h, 0, 0)),
        out_shape=jax.ShapeDtypeStruct((H, 2 * T, T), F32),
        name="bias_tiles",
    )(rel_bias, buckets)


def _inproj_kernel(x_ref, mod_ref, n1g_ref, wqT_ref, wk_ref, wvT_ref, wu_ref, wg_ref, wga_ref,
                   wgb_ref, lng_ref, lnb_ref, ws_ref, bs_ref,
                   qT_ref, k_ref, vT_ref, ob_ref, sga_ref, sgb_ref):
    tm, D = x_ref.shape
    NC = INPROJ_COLS
    chunks = [slice(c * NC, (c + 1) * NC) for c in range(D // NC)]
    x = x_ref[...]
    y = x * lax.rsqrt(jnp.mean(x * x, axis=-1, keepdims=True) + RMS_EPS) * n1g_ref[...]
    h = (y * (1.0 + mod_ref[1:2, :]) + mod_ref[0:1, :]).astype(BF16)

    rnd = [0]
    queue = []
    g_parts, u_parts, v_parts, stats = {}, {}, {}, {}

    def put(fn, *args):
        queue.append((rnd[0], functools.partial(fn, *args)))

    def epi_g(c, z):
        g = _gelu_tanh(z)
        g_parts[c] = g
        rs = jnp.sum(g, axis=-1, keepdims=True)
        stats["sum"] = rs if c == 0 else stats["sum"] + rs
        if c == len(chunks) - 1:
            for cc in range(len(chunks)):
                put(ln_center, cc)
            for cc in range(len(chunks)):
                put(ln_scale, cc)

    def ln_center(c):
        gc = g_parts[c] - stats["sum"] * (1.0 / D)
        g_parts[c] = gc
        sq = jnp.sum(gc * gc, axis=-1, keepdims=True)
        stats["sq"] = sq if c == 0 else stats["sq"] + sq

    def ln_scale(c):
        rstd = lax.rsqrt(stats["sq"] * (1.0 / D) + LN_EPS)
        v = g_parts.pop(c) * rstd * lng_ref[:, chunks[c]] + lnb_ref[:, chunks[c]]
        v_parts[c] = v.astype(BF16)

    def epi_u(c, z):
        u_parts[c] = _gelu_tanh(z)

    def epi_k(c, z):
        k_ref[:, chunks[c]] = z.astype(BF16)

    def epi_q(c, z):
        qz = (z * (LOG2E / math.sqrt(HEAD_DIM))).astype(BF16)
        T = qT_ref.shape[-1]
        for j in range(qT_ref.shape[0]):
            qT_ref[j, chunks[c], :] = qz[:, j * T:(j + 1) * T]

    def epi_v(c, z):
        vT_ref[chunks[c], :] = z.astype(BF16)

    def epi_gate(ref, c, z):
        ref[:, chunks[c]] = _sigmoid(z).astype(BF16)

    chunk_rows = [slice(ci * CHUNK, (ci + 1) * CHUNK) for ci in range(tm // CHUNK)]

    def group_cols(gi):
        lo = gi * GROUP_WIDTH
        return lo // NC, slice(lo % NC, lo % NC + GROUP_WIDTH)

    def epi_sv(gi, sv):
        c, local = group_cols(gi)
        cols = slice(gi * GROUP_WIDTH, (gi + 1) * GROUP_WIDTH)
        for ci, rows in enumerate(chunk_rows):
            sv_c = sv[:, ci * GROUP_WIDTH:(ci + 1) * GROUP_WIDTH]
            ob_ref[rows, cols] = (u_parts[c][rows, local] * sv_c).astype(BF16)

    def rows_dot(w_ref, cols):
        return jnp.dot(h, w_ref[:, cols], preferred_element_type=F32)

    def feat_dot(wT_ref, rows):
        return lax.dot_general(wT_ref[rows, :], h, _NT, preferred_element_type=F32)

    def matmuls():
        heavy = ([(epi_g, (c,), rows_dot, (wg_ref, s)) for c, s in enumerate(chunks)]
                 + [(epi_u, (c,), rows_dot, (wu_ref, s)) for c, s in enumerate(chunks)]
                 + [(epi_gate, (sga_ref, c), rows_dot, (wga_ref, s)) for c, s in enumerate(chunks)]
                 + [(epi_gate, (sgb_ref, c), rows_dot, (wgb_ref, s)) for c, s in enumerate(chunks)])
        light = ([(epi_k, (c,), rows_dot, (wk_ref, s)) for c, s in enumerate(chunks)]
                 + [(epi_q, (c,), feat_dot, (wqT_ref, s)) for c, s in enumerate(chunks)]
                 + [(epi_v, (c,), feat_dot, (wvT_ref, s)) for c, s in enumerate(chunks)])
        order = []
        while heavy or light:
            order += heavy[:1] + light[:1]
            heavy, light = heavy[1:], light[1:]
        for epi, epi_args, dot_fn, dot_args in order:
            put(epi, *epi_args, dot_fn(*dot_args))
            yield True
        tri = (lax.broadcasted_iota(jnp.int32, (CHUNK, CHUNK), 0)
               >= lax.broadcasted_iota(jnp.int32, (CHUNK, CHUNK), 1))
        for gi in range(ws_ref.shape[0]):
            ws = jnp.where(tri, ws_ref[gi], 0.0).astype(BF16)
            c, local = group_cols(gi)
            vcat = jnp.concatenate([v_parts[c][rows, local] for rows in chunk_rows], axis=1)
            sv = jnp.dot(ws, vcat, preferred_element_type=F32) + bs_ref[:, gi:gi + 1]
            put(epi_sv, gi, sv)
            yield True

    mm = matmuls()
    more = True
    while more or queue:
        if more:
            more = next(mm, None) is not None
        rnd[0] += 1
        ready = sum(1 for r, _ in queue if r < rnd[0] - 1) if more else len(queue)
        for _ in range(min(ready, 2 if more else ready)):
            queue.pop(0)[1]()


def _inproj(x, mod3, norm1_g, w_in, ln_v_g, ln_v_b, w_spatial, b_spatial):
    B, S, D = x.shape
    T = SEQ_TILE
    nt = S // T
    tm = INPROJ_TILE
    wq, wk, wv, wu, wg, wga, wgb = [w_in[:, i * D:(i + 1) * D].astype(BF16) for i in range(7)]
    wqT = wq.T
    wvT = wv.T
    G = w_spatial.shape[0]
    row = lambda a: a.reshape(1, D)
    tok = pl.BlockSpec((None, tm, D), lambda b, t: (b, t, 0))
    q_featT = pl.BlockSpec((None, tm // T, D, T), lambda b, t: (b, t, 0, 0))
    v_featT = pl.BlockSpec((None, D, tm), lambda b, t: (b, 0, t))
    out_tok = jax.ShapeDtypeStruct((B, S, D), BF16)
    return pl.pallas_call(
        _inproj_kernel,
        grid=(B, S // tm),
        in_specs=[tok,
                  pl.BlockSpec((None, 6, D), lambda b, t: (b, 0, 0)),
                  _resident((1, D)),
                  _resident((D, D)), _resident((D, D)), _resident((D, D)), _resident((D, D)),
                  _resident((D, D)), _resident((D, D)), _resident((D, D)),
                  _resident((1, D)), _resident((1, D)),
                  _resident((G, CHUNK, CHUNK)), _resident((CHUNK, G))],
        out_specs=[q_featT, tok, v_featT, tok, tok, tok],
        out_shape=[jax.ShapeDtypeStruct((B, nt, D, T), BF16), out_tok,
                   jax.ShapeDtypeStruct((B, D, S), BF16), out_tok, out_tok, out_tok],
        compiler_params=pltpu.CompilerParams(
            dimension_semantics=("arbitrary", "arbitrary"),
            vmem_limit_bytes=48 * 1024 * 1024),
        name="inproj",
    )(x, mod3, row(norm1_g), wqT, wk, wvT, wu, wg, wga, wgb, row(ln_v_g), row(ln_v_b),
      w_spatial, b_spatial.T)


def _attn_kernel(qT_ref, k_ref, vT_ref, tbl_ref, lam_ref, sg_ref, o_ref, va_ref):
    nt, _, T = qT_ref.shape
    feat = lax.broadcasted_iota(jnp.int32, (V_DIM, T), 0)
    va_ref[0:V_DIM, :] = vT_ref[...]
    va_ref[V_DIM:, :] = jnp.ones((ONES_ROWS, va_ref.shape[1]), BF16)
    lam = (jnp.exp(jnp.sum(lam_ref[0:1, :] * lam_ref[1:2, :], axis=-1, keepdims=True))
           - jnp.exp(jnp.sum(lam_ref[2:3, :] * lam_ref[3:4, :], axis=-1, keepdims=True))
           + LAM_INIT)

    units = [(i, mi) for i in reversed(range(nt)) for mi in range(2)]
    tick = [0]
    s_done = {}
    p_done = {}
    outs = {}

    def scores():
        for u in units:
            i, mi = u
            in_map = (feat < HEAD_DIM) if mi == 0 else (feat >= HEAD_DIM)
            qz = jnp.where(in_map, qT_ref[i].astype(F32), 0.0).astype(BF16)
            s_list, m = [], None
            for c in range(i + 1):
                s = jnp.dot(k_ref[c * T:(c + 1) * T, :], qz, preferred_element_type=F32)
                if c == i:
                    s = s + tbl_ref[T:, :]
                elif c == i - 1:
                    s = s + tbl_ref[:T, :]
                mc = jnp.max(s, axis=0, keepdims=True)
                m = mc if m is None else jnp.maximum(m, mc)
                s_list.append(s)
                yield True
            s_done[u] = (s_list, m, tick[0])

    def probs():
        for u in units:
            while u not in s_done or tick[0] < s_done[u][2] + STAGE_LAG:
                yield False
            s_list, m, _ = s_done.pop(u)
            for c, s in enumerate(s_list):
                p_done[u, c] = (jnp.exp2((s - m).astype(BF16)), tick[0])
                yield True

    def values():
        for u in units:
            i, mi = u
            acc = None
            for c in range(i + 1):
                while (u, c) not in p_done or tick[0] < p_done[u, c][1] + STAGE_LAG:
                    yield False
                p, _ = p_done.pop((u, c))
                d = jnp.dot(va_ref[:, c * T:(c + 1) * T], p, preferred_element_type=F32)
                acc = d if acc is None else acc + d
                yield True
            outs[u] = acc[0:V_DIM, :] * (1.0 / acc[V_DIM:V_DIM + 1, :])
            if mi == 1:
                oT = outs.pop((i, 0)) - lam * outs.pop((i, 1))
                yT = oT * lax.rsqrt(jnp.mean(oT * oT, axis=0, keepdims=True) + RMS_EPS)
                y = yT.T * sg_ref[...] * (1.0 - LAM_INIT)
                o_ref[i * T:(i + 1) * T, :] = y.astype(BF16)

    streams = [scores(), probs(), values()]
    while streams:
        streams = [g for g in streams if next(g, None) is not None]
        tick[0] += 1


def _attention(qT, k, vT, tbl, lam_vecs, subln_g):
    B, nt, D, T = qT.shape
    S = nt * T
    H = D // V_DIM
    return pl.pallas_call(
        _attn_kernel,
        grid=(B, H),
        in_specs=[pl.BlockSpec((None, nt, V_DIM, T), lambda b, h: (b, 0, h, 0)),
                  pl.BlockSpec((None, S, V_DIM), lambda b, h: (b, 0, h)),
                  pl.BlockSpec((None, V_DIM, S), lambda b, h: (b, h, 0)),
                  pl.BlockSpec((None, 2 * T, T), lambda b, h: (h, 0, 0)),
                  pl.BlockSpec((4, HEAD_DIM), lambda b, h: (0, 0)),
                  pl.BlockSpec((1, V_DIM), lambda b, h: (0, 0))],
        out_specs=pl.BlockSpec((None, S, V_DIM), lambda b, h: (b, 0, h)),
        out_shape=jax.ShapeDtypeStruct((B, S, D), BF16),
        scratch_shapes=[pltpu.VMEM((V_DIM + ONES_ROWS, S), BF16)],
        compiler_params=pltpu.CompilerParams(
            dimension_semantics=("arbitrary", "arbitrary"),
            vmem_limit_bytes=48 * 1024 * 1024),
        name="attn",
    )(qT, k, vT, tbl, lam_vecs, subln_g.reshape(1, V_DIM))


def _tail_kernel(x_ref, oa_ref, ob_ref, sga_ref, sgb_ref, mod_ref, n2g_ref, fg_ref,
                 wpa_ref, wpb_ref, wo_ref, wfi_ref, wfo_ref, o_ref):
    hidden = wfo_ref.shape[0]
    tm = x_ref.shape[0]
    bounds = list(range(0, hidden, TAIL_HIDDEN_CHUNK)) + [hidden]

    def row_group(rows):
        pa = jnp.dot(oa_ref[rows, :], wpa_ref[...], preferred_element_type=F32)
        pb = jnp.dot(ob_ref[rows, :], wpb_ref[...], preferred_element_type=F32)
        yield True
        merged = (sga_ref[rows, :].astype(F32) * pa + sgb_ref[rows, :].astype(F32) * pb)
        x1 = x_ref[rows, :] + mod_ref[2:3, :] * jnp.dot(merged.astype(BF16), wo_ref[...],
                                                        preferred_element_type=F32)
        yield True
        y = x1 * lax.rsqrt(jnp.mean(x1 * x1, axis=-1, keepdims=True) + RMS_EPS) * n2g_ref[...]
        h2 = (y * (1.0 + mod_ref[4:5, :]) + mod_ref[3:4, :]).astype(BF16)
        ffn = None
        for lo, hi in zip(bounds[:-1], bounds[1:]):
            gate = jnp.dot(h2, wfi_ref[:, lo:hi], preferred_element_type=F32)
            up = jnp.dot(h2, wfi_ref[:, hidden + lo:hidden + hi], preferred_element_type=F32)
            yield True
            act = (gate * _sigmoid(gate) * up).astype(BF16)
            part = jnp.dot(act, wfo_ref[lo:hi, :], preferred_element_type=F32)
            ffn = part if ffn is None else ffn + part
        yield True
        x2 = x1 + mod_ref[5:6, :] * ffn
        o_ref[rows, :] = (x2 * lax.rsqrt(jnp.mean(x2 * x2, axis=-1, keepdims=True) + RMS_EPS)
                          * fg_ref[...])

    live = [row_group(slice(r, r + TAIL_ROWS)) for r in range(0, tm, TAIL_ROWS)]
    while live:
        live = [g for g in live if next(g, None) is not None]


def _tail(x, oa, ob, sga, sgb, mod3, norm2_g, final_g, w_proj_a, w_proj_b, w_out, w_ffn_in,
          w_ffn_out):
    B, S, D = x.shape
    T = TAIL_TILE
    hidden = w_ffn_out.shape[0]
    row = lambda a: a.reshape(1, D)
    tok = pl.BlockSpec((None, T, D), lambda b, t: (b, t, 0))
    return pl.pallas_call(
        _tail_kernel,
        grid=(B, S // T),
        in_specs=[tok, tok, tok, tok, tok,
                  pl.BlockSpec((None, 6, D), lambda b, t: (b, 0, 0)),
                  _resident((1, D)), _resident((1, D)),
                  _resident((D, D)), _resident((D, D)), _resident((D, D)),
                  _resident((D, 2 * hidden)), _resident((hidden, D))],
        out_specs=tok,
        out_shape=jax.ShapeDtypeStruct((B, S, D), F32),
        compiler_params=pltpu.CompilerParams(
            dimension_semantics=("arbitrary", "arbitrary"),
            vmem_limit_bytes=58 * 1024 * 1024),
        name="tail",
    )(x, oa, ob, sga, sgb, mod3, row(norm2_g), row(final_g),
      w_proj_a.astype(BF16), w_proj_b.astype(BF16), w_out.astype(BF16),
      w_ffn_in.astype(BF16), w_ffn_out.astype(BF16))


def kernel(x, c, w_ada, b_ada, norm1_g, norm2_g, w_in, lambda_q1, lambda_k1, lambda_q2, lambda_k2,
           subln_g, ln_v_g, ln_v_b, w_spatial, b_spatial, w_proj_a, w_proj_b, w_out, w_ffn_in,
           w_ffn_out, rel_bias, final_g):
    B, S, D = x.shape
    assert w_ada.shape[0] == 1, "single-layer block"
    assert S % INPROJ_TILE == 0 and S % TAIL_TILE == 0
    assert INPROJ_TILE % SEQ_TILE == 0 and INPROJ_TILE % CHUNK == 0
    assert SEQ_TILE >= REL_MAX_DIST, "bias must be constant beyond the first off-diagonal tile"

    mod3 = _ada(c, w_ada[0], b_ada[0]).reshape(B, 6, D)
    tbl = _bias_tiles(rel_bias, SEQ_TILE)
    qT, k, vT, ob, sga, sgb = _inproj(x, mod3, norm1_g[0], w_in[0], ln_v_g[0], ln_v_b[0],
                                      w_spatial[0], b_spatial[0])
    lam_vecs = jnp.concatenate([lambda_q1, lambda_k1, lambda_q2, lambda_k2], axis=0)
    oa = _attention(qT, k, vT, tbl, lam_vecs, subln_g[0])
    return _tail(x, oa, ob, sga, sgb, mod3, norm2_g[0], final_g, w_proj_a[0], w_proj_b[0],
                 w_out[0], w_ffn_in[0], w_ffn_out[0])
```

```python
import functools
import math

import numpy as np
import jax
import jax.numpy as jnp
from jax import lax
from jax.experimental import pallas as pl
from jax.experimental.pallas import tpu as pltpu

F32 = jnp.float32
BF16 = jnp.bfloat16

RMS_EPS = 1e-6
LN_EPS = 1e-5
HEAD_DIM = 64
V_DIM = 2 * HEAD_DIM
CHUNK = 128
GROUP_WIDTH = 128
REL_BUCKETS = 32
REL_MAX_EXACT = REL_BUCKETS // 2
REL_MAX_DIST = 128
LAM_INIT = 0.8 - 0.6 * math.exp(-0.3 * 0)
MASK_VALUE = -1e30
LOG2E = math.log2(math.e)
ONES_ROWS = 16
KEY_CHUNK = 256
EXP_LAG = 8
VALUE_LAG = 2

SEQ_TILE = 256
INPROJ_TILE = 512
INPROJ_COLS = 256
TAIL_TILE = 512
TAIL_ROWS = 256
TAIL_HIDDEN_CHUNK = 1536
V7X_VMEM_BYTES = 64 * 1024 * 1024

_NT = (((1,), (1,)), ((), ()))


def _sigmoid(x):
    return 1.0 / (1.0 + jnp.exp2(x * (-LOG2E)))


def _gelu_tanh(x):
    a = -2.0 * math.sqrt(2.0 / math.pi) * LOG2E
    return x / (1.0 + jnp.exp2(x * (a + (a * 0.044715) * (x * x))))


def _resident(shape):
    n = len(shape)
    return pl.BlockSpec(shape, lambda *_: (0,) * n, pipeline_mode=pl.Buffered(1))


def _ada_kernel(c_ref, w_ref, b_ref, o_ref):
    c = c_ref[...]
    ca = c * _sigmoid(c)
    o_ref[...] = jnp.dot(ca, w_ref[...], precision=lax.Precision.HIGHEST,
                         preferred_element_type=F32) + b_ref[...]


def _ada(c, w_ada, b_ada):
    B, D = c.shape
    N = w_ada.shape[1]
    tn = 1024
    return pl.pallas_call(
        _ada_kernel,
        grid=(N // tn,),
        in_specs=[pl.BlockSpec((B, D), lambda n: (0, 0)),
                  pl.BlockSpec((D, tn), lambda n: (0, n)),
                  pl.BlockSpec((1, tn), lambda n: (0, n))],
        out_specs=pl.BlockSpec((B, tn), lambda n: (0, n)),
        out_shape=jax.ShapeDtypeStruct((B, N), F32),
        name="ada",
    )(c, w_ada, b_ada.reshape(1, N))


def _bucket_tiles(T):
    kk = np.arange(T, dtype=np.int32)[:, None]
    qq = np.arange(T, dtype=np.int32)[None, :]
    out = []
    for o in (1, 0):
        dist = o * T + qq - kk
        n = np.maximum(dist, 0)
        nf = np.maximum(n, 1).astype(np.float32)
        large = REL_MAX_EXACT + (np.log(nf / np.float32(REL_MAX_EXACT))
                                 / np.float32(math.log(REL_MAX_DIST / REL_MAX_EXACT))
                                 * np.float32(REL_BUCKETS - REL_MAX_EXACT)).astype(np.int32)
        large = np.minimum(large, REL_BUCKETS - 1)
        b = np.where(n < REL_MAX_EXACT, n, large)
        out.append(np.where(dist < 0, -1, b).astype(np.int32))
    return np.concatenate(out, axis=0)


def _bias_kernel(rb_ref, bucket_ref, o_ref):
    h = pl.program_id(0)
    bk = bucket_ref[...]
    far = rb_ref[REL_BUCKETS - 1, h]
    acc = jnp.where(bk < 0, MASK_VALUE, 0.0).astype(F32)
    for b in range(REL_BUCKETS - 1):
        acc = jnp.where(bk == b, (rb_ref[b, h] - far) * LOG2E, acc)
    o_ref[...] = acc


def _bias_tiles(rel_bias, T):
    H = rel_bias.shape[1]
    buckets = jnp.asarray(_bucket_tiles(T))
    return pl.pallas_call(
        _bias_kernel,
        grid=(H,),
        in_specs=[pl.BlockSpec(memory_space=pltpu.SMEM),
                  pl.BlockSpec((2 * T, T), lambda h: (0, 0))],
        out_specs=pl.BlockSpec((None, 2 * T, T), lambda h: (h, 0, 0)),
        out_shape=jax.ShapeDtypeStruct((H, 2 * T, T), F32),
        name="bias_tiles",
    )(rel_bias, buckets)


def _inproj_kernel(x_ref, mod_ref, n1g_ref, wqT_ref, wk_ref, wvT_ref, wu_ref, wg_ref, wga_ref,
                   wgb_ref, lng_ref, lnb_ref, ws_ref, bs_ref,
                   qT_ref, k_ref, vT_ref, ob_ref, sga_ref, sgb_ref):
    tm, D = x_ref.shape
    NC = INPROJ_COLS
    chunks = [slice(c * NC, (c + 1) * NC) for c in range(D // NC)]
    x = x_ref[...]
    y = x * lax.rsqrt(jnp.mean(x * x, axis=-1, keepdims=True) + RMS_EPS) * n1g_ref[...]
    h = (y * (1.0 + mod_ref[1:2, :]) + mod_ref[0:1, :]).astype(BF16)

    rnd = [0]
    queue = []
    g_parts, u_parts, v_parts, stats = {}, {}, {}, {}

    def put(fn, *args):
        queue.append((rnd[0], functools.partial(fn, *args)))

    def epi_g(c, z):
        g = _gelu_tanh(z)
        g_parts[c] = g
        rs = jnp.sum(g, axis=-1, keepdims=True)
        stats["sum"] = rs if c == 0 else stats["sum"] + rs
        if c == len(chunks) - 1:
            for cc in range(len(chunks)):
                put(ln_center, cc)
            for cc in range(len(chunks)):
                put(ln_scale, cc)

    def ln_center(c):
        gc = g_parts[c] - stats["sum"] * (1.0 / D)
        g_parts[c] = gc
        sq = jnp.sum(gc * gc, axis=-1, keepdims=True)
        stats["sq"] = sq if c == 0 else stats["sq"] + sq

    def ln_scale(c):
        rstd = lax.rsqrt(stats["sq"] * (1.0 / D) + LN_EPS)
        v = g_parts.pop(c) * rstd * lng_ref[:, chunks[c]] + lnb_ref[:, chunks[c]]
        v_parts[c] = v.astype(BF16)

    def epi_u(c, z):
        u_parts[c] = _gelu_tanh(z)

    def epi_k(c, z):
        k_ref[:, chunks[c]] = z.astype(BF16)

    def epi_q(c, z):
        qz = (z * (LOG2E / math.sqrt(HEAD_DIM))).astype(BF16)
        T = qT_ref.shape[-1]
        for j in range(qT_ref.shape[0]):
            qT_ref[j, chunks[c], :] = qz[:, j * T:(j + 1) * T]

    def epi_v(c, z):
        vT_ref[chunks[c], :] = z.astype(BF16)

    def epi_gate(ref, c, z):
        ref[:, chunks[c]] = _sigmoid(z).astype(BF16)

    chunk_rows = [slice(ci * CHUNK, (ci + 1) * CHUNK) for ci in range(tm // CHUNK)]

    def group_cols(gi):
        lo = gi * GROUP_WIDTH
        return lo // NC, slice(lo % NC, lo % NC + GROUP_WIDTH)

    def epi_sv(gi, sv):
        c, local = group_cols(gi)
        cols = slice(gi * GROUP_WIDTH, (gi + 1) * GROUP_WIDTH)
        for ci, rows in enumerate(chunk_rows):
            sv_c = sv[:, ci * GROUP_WIDTH:(ci + 1) * GROUP_WIDTH]
            ob_ref[rows, cols] = (u_parts[c][rows, local] * sv_c).astype(BF16)

    def rows_dot(w_ref, cols):
        return jnp.dot(h, w_ref[:, cols], preferred_element_type=F32)

    def feat_dot(wT_ref, rows):
        return lax.dot_general(wT_ref[rows, :], h, _NT, preferred_element_type=F32)

    def matmuls():
        heavy = ([(epi_g, (c,), rows_dot, (wg_ref, s)) for c, s in enumerate(chunks)]
                 + [(epi_u, (c,), rows_dot, (wu_ref, s)) for c, s in enumerate(chunks)]
                 + [(epi_gate, (sga_ref, c), rows_dot, (wga_ref, s)) for c, s in enumerate(chunks)]
                 + [(epi_gate, (sgb_ref, c), rows_dot, (wgb_ref, s)) for c, s in enumerate(chunks)])
        light = ([(epi_k, (c,), rows_dot, (wk_ref, s)) for c, s in enumerate(chunks)]
                 + [(epi_q, (c,), feat_dot, (wqT_ref, s)) for c, s in enumerate(chunks)]
                 + [(epi_v, (c,), feat_dot, (wvT_ref, s)) for c, s in enumerate(chunks)])
        order = []
        while heavy or light:
            order += heavy[:1] + light[:1]
            heavy, light = heavy[1:], light[1:]
        for epi, epi_args, dot_fn, dot_args in order:
            put(epi, *epi_args, dot_fn(*dot_args))
            yield True
        tri = (lax.broadcasted_iota(jnp.int32, (CHUNK, CHUNK), 0)
               >= lax.broadcasted_iota(jnp.int32, (CHUNK, CHUNK), 1))
        for gi in range(ws_ref.shape[0]):
            ws = jnp.where(tri, ws_ref[gi], 0.0).astype(BF16)
            c, local = group_cols(gi)
            vcat = jnp.concatenate([v_parts[c][rows, local] for rows in chunk_rows], axis=1)
            sv = jnp.dot(ws, vcat, preferred_element_type=F32) + bs_ref[:, gi:gi + 1]
            put(epi_sv, gi, sv)
            yield True

    mm = matmuls()
    more = True
    while more or queue:
        if more:
            more = next(mm, None) is not None
        rnd[0] += 1
        ready = sum(1 for r, _ in queue if r < rnd[0] - 1) if more else len(queue)
        for _ in range(min(ready, 2 if more else ready)):
            queue.pop(0)[1]()


def _inproj(x, mod3, norm1_g, w_in, ln_v_g, ln_v_b, w_spatial, b_spatial):
    B, S, D = x.shape
    T = SEQ_TILE
    nt = S // T
    tm = INPROJ_TILE
    wq, wk, wv, wu, wg, wga, wgb = [w_in[:, i * D:(i + 1) * D].astype(BF16) for i in range(7)]
    wqT = wq.T
    wvT = wv.T
    G = w_spatial.shape[0]
    row = lambda a: a.reshape(1, D)
    tok = pl.BlockSpec((None, tm, D), lambda b, t: (b, t, 0))
    q_featT = pl.BlockSpec((None, tm // T, D, T), lambda b, t: (b, t, 0, 0))
    v_featT = pl.BlockSpec((None, D, tm), lambda b, t: (b, 0, t))
    out_tok = jax.ShapeDtypeStruct((B, S, D), BF16)
    return pl.pallas_call(
        _inproj_kernel,
        grid=(B, S // tm),
        in_specs=[tok,
                  pl.BlockSpec((None, 6, D), lambda b, t: (b, 0, 0)),
                  _resident((1, D)),
                  _resident((D, D)), _resident((D, D)), _resident((D, D)), _resident((D, D)),
                  _resident((D, D)), _resident((D, D)), _resident((D, D)),
                  _resident((1, D)), _resident((1, D)),
                  _resident((G, CHUNK, CHUNK)), _resident((CHUNK, G))],
        out_specs=[q_featT, tok, v_featT, tok, tok, tok],
        out_shape=[jax.ShapeDtypeStruct((B, nt, D, T), BF16), out_tok,
                   jax.ShapeDtypeStruct((B, D, S), BF16), out_tok, out_tok, out_tok],
        compiler_params=pltpu.CompilerParams(
            dimension_semantics=("arbitrary", "arbitrary"),
            vmem_limit_bytes=48 * 1024 * 1024),
        name="inproj",
    )(x, mod3, row(norm1_g), wqT, wk, wvT, wu, wg, wga, wgb, row(ln_v_g), row(ln_v_b),
      w_spatial, b_spatial.T)


def _attn_kernel(qT_ref, k_ref, vT_ref, tbl_ref, lam_ref, sg_ref, o_ref, va_ref):
    nt, _, T = qT_ref.shape
    feat = lax.broadcasted_iota(jnp.int32, (V_DIM, T), 0)
    va_ref[0:V_DIM, :] = vT_ref[...]
    va_ref[V_DIM:, :] = jnp.ones((ONES_ROWS, va_ref.shape[1]), BF16)
    lam = (jnp.exp(jnp.sum(lam_ref[0:1, :] * lam_ref[1:2, :], axis=-1, keepdims=True))
           - jnp.exp(jnp.sum(lam_ref[2:3, :] * lam_ref[3:4, :], axis=-1, keepdims=True))
           + LAM_INIT)

    units = [(i, mi) for i in reversed(range(nt)) for mi in range(2)]
    tick = [0]
    s_done = {}
    p_done = {}
    outs = {}

    def key_chunks(i):
        near_lo, hi = max(i - 1, 0) * T, (i + 1) * T
        cuts = sorted(set(range(near_lo, -1, -KEY_CHUNK)) | {0})
        far = [(lo, nxt, None) for lo, nxt in zip(cuts[:-1], cuts[1:])]
        return far + [(near_lo, hi, slice(2 * T - (hi - near_lo), 2 * T))]

    def scores():
        for u in units:
            i, mi = u
            in_map = (feat < HEAD_DIM) if mi == 0 else (feat >= HEAD_DIM)
            qz = jnp.where(in_map, qT_ref[i].astype(F32), 0.0).astype(BF16)
            for c, (lo, hi, tbl_rows) in enumerate(key_chunks(i)):
                s = jnp.dot(k_ref[lo:hi, :], qz, preferred_element_type=F32)
                if tbl_rows is not None:
                    s = s + tbl_ref[tbl_rows, :]
                s_done[u, c] = (s, tick[0])
                yield True

    def probs():
        for u in units:
            for c in range(len(key_chunks(u[0]))):
                while (u, c) not in s_done or tick[0] < s_done[u, c][1] + EXP_LAG:
                    yield False
                s, _ = s_done.pop((u, c))
                mc = jnp.max(s, axis=0, keepdims=True)
                p_done[u, c] = (jnp.exp2((s - mc).astype(BF16)), mc, tick[0])
                yield True

    def values():
        for u in units:
            i, mi = u
            parts = []
            for c, (lo, hi, _) in enumerate(key_chunks(i)):
                while (u, c) not in p_done or tick[0] < p_done[u, c][2] + VALUE_LAG:
                    yield False
                p, mc, _ = p_done.pop((u, c))
                parts.append((jnp.dot(va_ref[:, lo:hi], p, preferred_element_type=F32), mc))
                yield True
            m = functools.reduce(jnp.maximum, [mc for _, mc in parts])
            acc = functools.reduce(lambda a, b: a + b, [d * jnp.exp2(mc - m) for d, mc in parts])
            outs[u] = acc[0:V_DIM, :] * (1.0 / acc[V_DIM:V_DIM + 1, :])
            if mi == 1:
                oT = outs.pop((i, 0)) - lam * outs.pop((i, 1))
                yT = oT * lax.rsqrt(jnp.mean(oT * oT, axis=0, keepdims=True) + RMS_EPS)
                y = yT.T * sg_ref[...] * (1.0 - LAM_INIT)
                o_ref[i * T:(i + 1) * T, :] = y.astype(BF16)

    streams = [scores(), probs(), values()]
    while streams:
        streams = [g for g in streams if next(g, None) is not None]
        tick[0] += 1


def _attention(qT, k, vT, tbl, lam_vecs, subln_g):
    B, nt, D, T = qT.shape
    S = nt * T
    H = D // V_DIM
    return pl.pallas_call(
        _attn_kernel,
        grid=(B, H),
        in_specs=[pl.BlockSpec((None, nt, V_DIM, T), lambda b, h: (b, 0, h, 0)),
                  pl.BlockSpec((None, S, V_DIM), lambda b, h: (b, 0, h)),
                  pl.BlockSpec((None, V_DIM, S), lambda b, h: (b, h, 0)),
                  pl.BlockSpec((None, 2 * T, T), lambda b, h: (h, 0, 0)),
                  pl.BlockSpec((4, HEAD_DIM), lambda b, h: (0, 0)),
                  pl.BlockSpec((1, V_DIM), lambda b, h: (0, 0))],
        out_specs=pl.BlockSpec((None, S, V_DIM), lambda b, h: (b, 0, h)),
        out_shape=jax.ShapeDtypeStruct((B, S, D), BF16),
        scratch_shapes=[pltpu.VMEM((V_DIM + ONES_ROWS, S), BF16)],
        compiler_params=pltpu.CompilerParams(
            dimension_semantics=("arbitrary", "arbitrary"),
            vmem_limit_bytes=48 * 1024 * 1024),
        name="attn",
    )(qT, k, vT, tbl, lam_vecs, subln_g.reshape(1, V_DIM))


def _tail_kernel(x_ref, oa_ref, ob_ref, sga_ref, sgb_ref, mod_ref, n2g_ref, fg_ref,
                 wpa_ref, wpb_ref, wo_ref, wfi_ref, wfo_ref, o_ref):
    hidden = wfo_ref.shape[0]
    tm = x_ref.shape[0]
    bounds = list(range(0, hidden, TAIL_HIDDEN_CHUNK)) + [hidden]

    def row_group(rows):
        pa = jnp.dot(oa_ref[rows, :], wpa_ref[...], preferred_element_type=F32)
        pb = jnp.dot(ob_ref[rows, :], wpb_ref[...], preferred_element_type=F32)
        yield True
        merged = (sga_ref[rows, :].astype(F32) * pa + sgb_ref[rows, :].astype(F32) * pb)
        x1 = x_ref[rows, :] + mod_ref[2:3, :] * jnp.dot(merged.astype(BF16), wo_ref[...],
                                                        preferred_element_type=F32)
        yield True
        y = x1 * lax.rsqrt(jnp.mean(x1 * x1, axis=-1, keepdims=True) + RMS_EPS) * n2g_ref[...]
        h2 = (y * (1.0 + mod_ref[4:5, :]) + mod_ref[3:4, :]).astype(BF16)
        ffn = None
        for lo, hi in zip(bounds[:-1], bounds[1:]):
            gate = jnp.dot(h2, wfi_ref[:, lo:hi], preferred_element_type=F32)
            up = jnp.dot(h2, wfi_ref[:, hidden + lo:hidden + hi], preferred_element_type=F32)
            yield True
            act = (gate * _sigmoid(gate) * up).astype(BF16)
            part = jnp.dot(act, wfo_ref[lo:hi, :], preferred_element_type=F32)
            ffn = part if ffn is None else ffn + part
        yield True
        x2 = x1 + mod_ref[5:6, :] * ffn
        o_ref[rows, :] = (x2 * lax.rsqrt(jnp.mean(x2 * x2, axis=-1, keepdims=True) + RMS_EPS)
                          * fg_ref[...])

    live = [row_group(slice(r, r + TAIL_ROWS)) for r in range(0, tm, TAIL_ROWS)]
    while live:
        live = [g for g in live if next(g, None) is not None]


def _tail(x, oa, ob, sga, sgb, mod3, norm2_g, final_g, w_proj_a, w_proj_b, w_out, w_ffn_in,
          w_ffn_out):
    B, S, D = x.shape
    T = TAIL_TILE
    hidden = w_ffn_out.shape[0]
    row = lambda a: a.reshape(1, D)
    tok = pl.BlockSpec((None, T, D), lambda b, t: (b, t, 0))
    return pl.pallas_call(
        _tail_kernel,
        grid=(B, S // T),
        in_specs=[tok, tok, tok, tok, tok,
                  pl.BlockSpec((None, 6, D), lambda b, t: (b, 0, 0)),
                  _resident((1, D)), _resident((1, D)),
                  _resident((D, D)), _resident((D, D)), _resident((D, D)),
                  _resident((D, 2 * hidden)), _resident((hidden, D))],
        out_specs=tok,
        out_shape=jax.ShapeDtypeStruct((B, S, D), F32),
        compiler_params=pltpu.CompilerParams(
            dimension_semantics=("arbitrary", "arbitrary"),
            vmem_limit_bytes=58 * 1024 * 1024),
        name="tail",
    )(x, oa, ob, sga, sgb, mod3, row(norm2_g), row(final_g),
      w_proj_a.astype(BF16), w_proj_b.astype(BF16), w_out.astype(BF16),
      w_ffn_in.astype(BF16), w_ffn_out.astype(BF16))


def kernel(x, c, w_ada, b_ada, norm1_g, norm2_g, w_in, lambda_q1, lambda_k1, lambda_q2, lambda_k2,
           subln_g, ln_v_g, ln_v_b, w_spatial, b_spatial, w_proj_a, w_proj_b, w_out, w_ffn_in,
           w_ffn_out, rel_bias, final_g):
    B, S, D = x.shape
    assert w_ada.shape[0] == 1, "single-layer block"
    assert S % INPROJ_TILE == 0 and S % TAIL_TILE == 0
    assert INPROJ_TILE % SEQ_TILE == 0 and INPROJ_TILE % CHUNK == 0
    assert SEQ_TILE >= REL_MAX_DIST, "bias must be constant beyond the first off-diagonal tile"

    mod3 = _ada(c, w_ada[0], b_ada[0]).reshape(B, 6, D)
    tbl = _bias_tiles(rel_bias, SEQ_TILE)
    qT, k, vT, ob, sga, sgb = _inproj(x, mod3, norm1_g[0], w_in[0], ln_v_g[0], ln_v_b[0],
                                      w_spatial[0], b_spatial[0])
    lam_vecs = jnp.concatenate([lambda_q1, lambda_k1, lambda_q2, lambda_k2], axis=0)
    oa = _attention(qT, k, vT, tbl, lam_vecs, subln_g[0])
    return _tail(x, oa, ob, sga, sgb, mod3, norm2_g[0], final_g, w_proj_a[0], w_proj_b[0],
                 w_out[0], w_ffn_in[0], w_ffn_out[0])
```

```python
import functools
import math

import numpy as np
import jax
import jax.numpy as jnp
from jax import lax
from jax.experimental import pallas as pl
from jax.experimental.pallas import tpu as pltpu

F32 = jnp.float32
BF16 = jnp.bfloat16

RMS_EPS = 1e-6
LN_EPS = 1e-5
HEAD_DIM = 64
V_DIM = 2 * HEAD_DIM
CHUNK = 128
GROUP_WIDTH = 128
REL_BUCKETS = 32
REL_MAX_EXACT = REL_BUCKETS // 2
REL_MAX_DIST = 128
LAM_INIT = 0.8 - 0.6 * math.exp(-0.3 * 0)
MASK_VALUE = -1e30
LOG2E = math.log2(math.e)
ONES_ROWS = 16
ATTN_HEADS_PER_STEP = 2
KEY_CHUNK = 256
EXP_LAG = 8
VALUE_LAG = 2

SEQ_TILE = 256
INPROJ_TILE = 512
INPROJ_COLS = 256
TAIL_TILE = 512
TAIL_ROWS = 256
TAIL_HIDDEN_CHUNK = 1536
V7X_VMEM_BYTES = 64 * 1024 * 1024

_NT = (((1,), (1,)), ((), ()))


def _sigmoid(x):
    return 1.0 / (1.0 + jnp.exp2(x * (-LOG2E)))


def _gelu_tanh(x):
    a = -2.0 * math.sqrt(2.0 / math.pi) * LOG2E
    return x / (1.0 + jnp.exp2(x * (a + (a * 0.044715) * (x * x))))


def _resident(shape):
    n = len(shape)
    return pl.BlockSpec(shape, lambda *_: (0,) * n, pipeline_mode=pl.Buffered(1))


def _ada_kernel(c_ref, w_ref, b_ref, o_ref):
    c = c_ref[...]
    ca = c * _sigmoid(c)
    o_ref[...] = jnp.dot(ca, w_ref[...], precision=lax.Precision.HIGHEST,
                         preferred_element_type=F32) + b_ref[...]


def _ada(c, w_ada, b_ada):
    B, D = c.shape
    N = w_ada.shape[1]
    tn = 1024
    return pl.pallas_call(
        _ada_kernel,
        grid=(N // tn,),
        in_specs=[pl.BlockSpec((B, D), lambda n: (0, 0)),
                  pl.BlockSpec((D, tn), lambda n: (0, n)),
                  pl.BlockSpec((1, tn), lambda n: (0, n))],
        out_specs=pl.BlockSpec((B, tn), lambda n: (0, n)),
        out_shape=jax.ShapeDtypeStruct((B, N), F32),
        name="ada",
    )(c, w_ada, b_ada.reshape(1, N))


def _bucket_tiles(T):
    kk = np.arange(T, dtype=np.int32)[:, None]
    qq = np.arange(T, dtype=np.int32)[None, :]
    out = []
    for o in (1, 0):
        dist = o * T + qq - kk
        n = np.maximum(dist, 0)
        nf = np.maximum(n, 1).astype(np.float32)
        large = REL_MAX_EXACT + (np.log(nf / np.float32(REL_MAX_EXACT))
                                 / np.float32(math.log(REL_MAX_DIST / REL_MAX_EXACT))
                                 * np.float32(REL_BUCKETS - REL_MAX_EXACT)).astype(np.int32)
        large = np.minimum(large, REL_BUCKETS - 1)
        b = np.where(n < REL_MAX_EXACT, n, large)
        out.append(np.where(dist < 0, -1, b).astype(np.int32))
    return np.concatenate(out, axis=0)


def _bias_kernel(rb_ref, bucket_ref, o_ref):
    h = pl.program_id(0)
    bk = bucket_ref[...]
    far = rb_ref[REL_BUCKETS - 1, h]
    acc = jnp.where(bk < 0, MASK_VALUE, 0.0).astype(F32)
    for b in range(REL_BUCKETS - 1):
        acc = jnp.where(bk == b, (rb_ref[b, h] - far) * LOG2E, acc)
    o_ref[...] = acc


def _bias_tiles(rel_bias, T):
    H = rel_bias.shape[1]
    buckets = jnp.asarray(_bucket_tiles(T))
    return pl.pallas_call(
        _bias_kernel,
        grid=(H,),
        in_specs=[pl.BlockSpec(memory_space=pltpu.SMEM),
                  pl.BlockSpec((2 * T, T), lambda h: (0, 0))],
        out_specs=pl.BlockSpec((None, 2 * T, T), lambda h: (h, 0, 0)),
        out_shape=jax.ShapeDtypeStruct((H, 2 * T, T), F32),
        name="bias_tiles",
    )(rel_bias, buckets)


def _inproj_kernel(x_ref, mod_ref, n1g_ref, wqT_ref, wk_ref, wvT_ref, wu_ref, wg_ref, wga_ref,
                   wgb_ref, lng_ref, lnb_ref, ws_ref, bs_ref,
                   qT_ref, k_ref, vT_ref, ob_ref, sga_ref, sgb_ref):
    tm, D = x_ref.shape
    NC = INPROJ_COLS
    chunks = [slice(c * NC, (c + 1) * NC) for c in range(D // NC)]
    x = x_ref[...]
    y = x * lax.rsqrt(jnp.mean(x * x, axis=-1, keepdims=True) + RMS_EPS) * n1g_ref[...]
    h = (y * (1.0 + mod_ref[1:2, :]) + mod_ref[0:1, :]).astype(BF16)

    rnd = [0]
    queue = []
    g_parts, u_parts, v_parts, stats = {}, {}, {}, {}

    def put(fn, *args):
        queue.append((rnd[0], functools.partial(fn, *args)))

    def epi_g(c, z):
        g = _gelu_tanh(z)
        g_parts[c] = g
        rs = jnp.sum(g, axis=-1, keepdims=True)
        stats["sum"] = rs if c == 0 else stats["sum"] + rs
        if c == len(chunks) - 1:
            for cc in range(len(chunks)):
                put(ln_center, cc)
            for cc in range(len(chunks)):
                put(ln_scale, cc)

    def ln_center(c):
        gc = g_parts[c] - stats["sum"] * (1.0 / D)
        g_parts[c] = gc
        sq = jnp.sum(gc * gc, axis=-1, keepdims=True)
        stats["sq"] = sq if c == 0 else stats["sq"] + sq

    def ln_scale(c):
        rstd = lax.rsqrt(stats["sq"] * (1.0 / D) + LN_EPS)
        v = g_parts.pop(c) * rstd * lng_ref[:, chunks[c]] + lnb_ref[:, chunks[c]]
        v_parts[c] = v.astype(BF16)

    def epi_u(c, z):
        u_parts[c] = _gelu_tanh(z)

    def epi_k(c, z):
        k_ref[:, chunks[c]] = z.astype(BF16)

    def epi_q(c, z):
        qz = (z * (LOG2E / math.sqrt(HEAD_DIM))).astype(BF16)
        T = qT_ref.shape[-1]
        for j in range(qT_ref.shape[0]):
            qT_ref[j, chunks[c], :] = qz[:, j * T:(j + 1) * T]

    def epi_v(c, z):
        vT_ref[chunks[c], :] = z.astype(BF16)

    def epi_gate(ref, c, z):
        ref[:, chunks[c]] = _sigmoid(z).astype(BF16)

    chunk_rows = [slice(ci * CHUNK, (ci + 1) * CHUNK) for ci in range(tm // CHUNK)]

    def group_cols(gi):
        lo = gi * GROUP_WIDTH
        return lo // NC, slice(lo % NC, lo % NC + GROUP_WIDTH)

    def epi_sv(gi, sv):
        c, local = group_cols(gi)
        cols = slice(gi * GROUP_WIDTH, (gi + 1) * GROUP_WIDTH)
        for ci, rows in enumerate(chunk_rows):
            sv_c = sv[:, ci * GROUP_WIDTH:(ci + 1) * GROUP_WIDTH]
            ob_ref[rows, cols] = (u_parts[c][rows, local] * sv_c).astype(BF16)

    def rows_dot(w_ref, cols):
        return jnp.dot(h, w_ref[:, cols], preferred_element_type=F32)

    def feat_dot(wT_ref, rows):
        return lax.dot_general(wT_ref[rows, :], h, _NT, preferred_element_type=F32)

    def matmuls():
        heavy = ([(epi_g, (c,), rows_dot, (wg_ref, s)) for c, s in enumerate(chunks)]
                 + [(epi_u, (c,), rows_dot, (wu_ref, s)) for c, s in enumerate(chunks)]
                 + [(epi_gate, (sga_ref, c), rows_dot, (wga_ref, s)) for c, s in enumerate(chunks)]
                 + [(epi_gate, (sgb_ref, c), rows_dot, (wgb_ref, s)) for c, s in enumerate(chunks)])
        light = ([(epi_k, (c,), rows_dot, (wk_ref, s)) for c, s in enumerate(chunks)]
                 + [(epi_q, (c,), feat_dot, (wqT_ref, s)) for c, s in enumerate(chunks)]
                 + [(epi_v, (c,), feat_dot, (wvT_ref, s)) for c, s in enumerate(chunks)])
        order = []
        while heavy or light:
            order += heavy[:1] + light[:1]
            heavy, light = heavy[1:], light[1:]
        for epi, epi_args, dot_fn, dot_args in order:
            put(epi, *epi_args, dot_fn(*dot_args))
            yield True
        tri = (lax.broadcasted_iota(jnp.int32, (CHUNK, CHUNK), 0)
               >= lax.broadcasted_iota(jnp.int32, (CHUNK, CHUNK), 1))
        for gi in range(ws_ref.shape[0]):
            ws = jnp.where(tri, ws_ref[gi], 0.0).astype(BF16)
            c, local = group_cols(gi)
            vcat = jnp.concatenate([v_parts[c][rows, local] for rows in chunk_rows], axis=1)
            sv = jnp.dot(ws, vcat, preferred_element_type=F32) + bs_ref[:, gi:gi + 1]
            put(epi_sv, gi, sv)
            yield True

    mm = matmuls()
    more = True
    while more or queue:
        if more:
            more = next(mm, None) is not None
        rnd[0] += 1
        ready = sum(1 for r, _ in queue if r < rnd[0] - 1) if more else len(queue)
        for _ in range(min(ready, 2 if more else ready)):
            queue.pop(0)[1]()


def _inproj(x, mod3, norm1_g, w_in, ln_v_g, ln_v_b, w_spatial, b_spatial):
    B, S, D = x.shape
    T = SEQ_TILE
    nt = S // T
    tm = INPROJ_TILE
    wq, wk, wv, wu, wg, wga, wgb = [w_in[:, i * D:(i + 1) * D].astype(BF16) for i in range(7)]
    wqT = wq.T
    wvT = wv.T
    G = w_spatial.shape[0]
    row = lambda a: a.reshape(1, D)
    tok = pl.BlockSpec((None, tm, D), lambda b, t: (b, t, 0))
    q_featT = pl.BlockSpec((None, tm // T, D, T), lambda b, t: (b, t, 0, 0))
    v_featT = pl.BlockSpec((None, D, tm), lambda b, t: (b, 0, t))
    out_tok = jax.ShapeDtypeStruct((B, S, D), BF16)
    return pl.pallas_call(
        _inproj_kernel,
        grid=(B, S // tm),
        in_specs=[tok,
                  pl.BlockSpec((None, 6, D), lambda b, t: (b, 0, 0)),
                  _resident((1, D)),
                  _resident((D, D)), _resident((D, D)), _resident((D, D)), _resident((D, D)),
                  _resident((D, D)), _resident((D, D)), _resident((D, D)),
                  _resident((1, D)), _resident((1, D)),
                  _resident((G, CHUNK, CHUNK)), _resident((CHUNK, G))],
        out_specs=[q_featT, tok, v_featT, tok, tok, tok],
        out_shape=[jax.ShapeDtypeStruct((B, nt, D, T), BF16), out_tok,
                   jax.ShapeDtypeStruct((B, D, S), BF16), out_tok, out_tok, out_tok],
        compiler_params=pltpu.CompilerParams(
            dimension_semantics=("arbitrary", "arbitrary"),
            vmem_limit_bytes=48 * 1024 * 1024),
        name="inproj",
    )(x, mod3, row(norm1_g), wqT, wk, wvT, wu, wg, wga, wgb, row(ln_v_g), row(ln_v_b),
      w_spatial, b_spatial.T)


def _attn_kernel(qT_ref, k_ref, vT_ref, tbl_ref, lam_ref, sg_ref, o_ref, va_ref):
    nt, _, T = qT_ref.shape
    n_heads = va_ref.shape[0]
    feat = lax.broadcasted_iota(jnp.int32, (V_DIM, T), 0)
    head_cols = [slice(hl * V_DIM, (hl + 1) * V_DIM) for hl in range(n_heads)]
    for hl in range(n_heads):
        va_ref[hl, 0:V_DIM, :] = vT_ref[head_cols[hl], :]
        va_ref[hl, V_DIM:, :] = jnp.ones((ONES_ROWS, va_ref.shape[2]), BF16)
    lam = (jnp.exp(jnp.sum(lam_ref[0:1, :] * lam_ref[1:2, :], axis=-1, keepdims=True))
           - jnp.exp(jnp.sum(lam_ref[2:3, :] * lam_ref[3:4, :], axis=-1, keepdims=True))
           + LAM_INIT)

    units = [(hl, i, mi) for i in reversed(range(nt)) for hl in range(n_heads) for mi in range(2)]
    tick = [0]
    s_done = {}
    p_done = {}
    outs = {}

    def key_chunks(i):
        near_lo, hi = max(i - 1, 0) * T, (i + 1) * T
        cuts = sorted(set(range(near_lo, -1, -KEY_CHUNK)) | {0})
        far = [(lo, nxt, None) for lo, nxt in zip(cuts[:-1], cuts[1:])]
        return far + [(near_lo, hi, slice(2 * T - (hi - near_lo), 2 * T))]

    def scores():
        for u in units:
            hl, i, mi = u
            in_map = (feat < HEAD_DIM) if mi == 0 else (feat >= HEAD_DIM)
            qz = jnp.where(in_map, qT_ref[i, head_cols[hl], :].astype(F32), 0.0).astype(BF16)
            for c, (lo, hi, tbl_rows) in enumerate(key_chunks(i)):
                s = jnp.dot(k_ref[lo:hi, head_cols[hl]], qz, preferred_element_type=F32)
                if tbl_rows is not None:
                    s = s + tbl_ref[hl, tbl_rows, :]
                s_done[u, c] = (s, tick[0])
                yield True

    def probs():
        for u in units:
            for c in range(len(key_chunks(u[1]))):
                while (u, c) not in s_done or tick[0] < s_done[u, c][1] + EXP_LAG:
                    yield False
                s, _ = s_done.pop((u, c))
                mc = jnp.max(s, axis=0, keepdims=True)
                p_done[u, c] = (jnp.exp2((s - mc).astype(BF16)), mc, tick[0])
                yield True

    def values():
        for u in units:
            hl, i, mi = u
            parts = []
            for c, (lo, hi, _) in enumerate(key_chunks(i)):
                while (u, c) not in p_done or tick[0] < p_done[u, c][2] + VALUE_LAG:
                    yield False
                p, mc, _ = p_done.pop((u, c))
                parts.append((jnp.dot(va_ref[hl, :, lo:hi], p, preferred_element_type=F32), mc))
                yield True
            m = functools.reduce(jnp.maximum, [mc for _, mc in parts])
            acc = functools.reduce(lambda a, b: a + b, [d * jnp.exp2(mc - m) for d, mc in parts])
            outs[u] = acc[0:V_DIM, :] * (1.0 / acc[V_DIM:V_DIM + 1, :])
            if mi == 1:
                oT = outs.pop((hl, i, 0)) - lam * outs.pop((hl, i, 1))
                yT = oT * lax.rsqrt(jnp.mean(oT * oT, axis=0, keepdims=True) + RMS_EPS)
                y = yT.T * sg_ref[...] * (1.0 - LAM_INIT)
                o_ref[i * T:(i + 1) * T, head_cols[hl]] = y.astype(BF16)

    streams = [scores(), probs(), values()]
    while streams:
        streams = [g for g in streams if next(g, None) is not None]
        tick[0] += 1


def _attention(qT, k, vT, tbl, lam_vecs, subln_g):
    B, nt, D, T = qT.shape
    S = nt * T
    H = D // V_DIM
    hp = ATTN_HEADS_PER_STEP
    W = hp * V_DIM
    return pl.pallas_call(
        _attn_kernel,
        grid=(B, H // hp),
        in_specs=[pl.BlockSpec((None, nt, W, T), lambda b, h: (b, 0, h, 0)),
                  pl.BlockSpec((None, S, W), lambda b, h: (b, 0, h)),
                  pl.BlockSpec((None, W, S), lambda b, h: (b, h, 0)),
                  pl.BlockSpec((hp, 2 * T, T), lambda b, h: (h, 0, 0)),
                  pl.BlockSpec((4, HEAD_DIM), lambda b, h: (0, 0)),
                  pl.BlockSpec((1, V_DIM), lambda b, h: (0, 0))],
        out_specs=pl.BlockSpec((None, S, W), lambda b, h: (b, 0, h)),
        out_shape=jax.ShapeDtypeStruct((B, S, D), BF16),
        scratch_shapes=[pltpu.VMEM((hp, V_DIM + ONES_ROWS, S), BF16)],
        compiler_params=pltpu.CompilerParams(
            dimension_semantics=("arbitrary", "arbitrary"),
            vmem_limit_bytes=48 * 1024 * 1024),
        name="attn",
    )(qT, k, vT, tbl, lam_vecs, subln_g.reshape(1, V_DIM))


def _tail_kernel(x_ref, oa_ref, ob_ref, sga_ref, sgb_ref, mod_ref, n2g_ref, fg_ref,
                 wpa_ref, wpb_ref, wo_ref, wfi_ref, wfo_ref, o_ref):
    hidden = wfo_ref.shape[0]
    tm = x_ref.shape[0]
    bounds = list(range(0, hidden, TAIL_HIDDEN_CHUNK)) + [hidden]

    def row_group(rows):
        pa = jnp.dot(oa_ref[rows, :], wpa_ref[...], preferred_element_type=F32)
        pb = jnp.dot(ob_ref[rows, :], wpb_ref[...], preferred_element_type=F32)
        yield True
        merged = (sga_ref[rows, :].astype(F32) * pa + sgb_ref[rows, :].astype(F32) * pb)
        x1 = x_ref[rows, :] + mod_ref[2:3, :] * jnp.dot(merged.astype(BF16), wo_ref[...],
                                                        preferred_element_type=F32)
        yield True
        y = x1 * lax.rsqrt(jnp.mean(x1 * x1, axis=-1, keepdims=True) + RMS_EPS) * n2g_ref[...]
        h2 = (y * (1.0 + mod_ref[4:5, :]) + mod_ref[3:4, :]).astype(BF16)
        ffn = None
        for lo, hi in zip(bounds[:-1], bounds[1:]):
            gate = jnp.dot(h2, wfi_ref[:, lo:hi], preferred_element_type=F32)
            up = jnp.dot(h2, wfi_ref[:, hidden + lo:hidden + hi], preferred_element_type=F32)
            yield True
            act = (gate * _sigmoid(gate) * up).astype(BF16)
            part = jnp.dot(act, wfo_ref[lo:hi, :], preferred_element_type=F32)
            ffn = part if ffn is None else ffn + part
        yield True
        x2 = x1 + mod_ref[5:6, :] * ffn
        o_ref[rows, :] = (x2 * lax.rsqrt(jnp.mean(x2 * x2, axis=-1, keepdims=True) + RMS_EPS)
                          * fg_ref[...])

    live = [row_group(slice(r, r + TAIL_ROWS)) for r in range(0, tm, TAIL_ROWS)]
    while live:
        live = [g for g in live if next(g, None) is not None]


def _tail(x, oa, ob, sga, sgb, mod3, norm2_g, final_g, w_proj_a, w_proj_b, w_out, w_ffn_in,
          w_ffn_out):
    B, S, D = x.shape
    T = TAIL_TILE
    hidden = w_ffn_out.shape[0]
    row = lambda a: a.reshape(1, D)
    tok = pl.BlockSpec((None, T, D), lambda b, t: (b, t, 0))
    return pl.pallas_call(
        _tail_kernel,
        grid=(B, S // T),
        in_specs=[tok, tok, tok, tok, tok,
                  pl.BlockSpec((None, 6, D), lambda b, t: (b, 0, 0)),
                  _resident((1, D)), _resident((1, D)),
                  _resident((D, D)), _resident((D, D)), _resident((D, D)),
                  _resident((D, 2 * hidden)), _resident((hidden, D))],
        out_specs=tok,
        out_shape=jax.ShapeDtypeStruct((B, S, D), F32),
        compiler_params=pltpu.CompilerParams(
            dimension_semantics=("arbitrary", "arbitrary"),
            vmem_limit_bytes=58 * 1024 * 1024),
        name="tail",
    )(x, oa, ob, sga, sgb, mod3, row(norm2_g), row(final_g),
      w_proj_a.astype(BF16), w_proj_b.astype(BF16), w_out.astype(BF16),
      w_ffn_in.astype(BF16), w_ffn_out.astype(BF16))


def kernel(x, c, w_ada, b_ada, norm1_g, norm2_g, w_in, lambda_q1, lambda_k1, lambda_q2, lambda_k2,
           subln_g, ln_v_g, ln_v_b, w_spatial, b_spatial, w_proj_a, w_proj_b, w_out, w_ffn_in,
           w_ffn_out, rel_bias, final_g):
    B, S, D = x.shape
    assert w_ada.shape[0] == 1, "single-layer block"
    assert S % INPROJ_TILE == 0 and S % TAIL_TILE == 0
    assert INPROJ_TILE % SEQ_TILE == 0 and INPROJ_TILE % CHUNK == 0
    assert SEQ_TILE >= REL_MAX_DIST, "bias must be constant beyond the first off-diagonal tile"

    mod3 = _ada(c, w_ada[0], b_ada[0]).reshape(B, 6, D)
    tbl = _bias_tiles(rel_bias, SEQ_TILE)
    qT, k, vT, ob, sga, sgb = _inproj(x, mod3, norm1_g[0], w_in[0], ln_v_g[0], ln_v_b[0],
                                      w_spatial[0], b_spatial[0])
    lam_vecs = jnp.concatenate([lambda_q1, lambda_k1, lambda_q2, lambda_k2], axis=0)
    oa = _attention(qT, k, vT, tbl, lam_vecs, subln_g[0])
    return _tail(x, oa, ob, sga, sgb, mod3, norm2_g[0], final_g, w_proj_a[0], w_proj_b[0],
                 w_out[0], w_ffn_in[0], w_ffn_out[0])
```

```python
import functools
import math

import numpy as np
import jax
import jax.numpy as jnp
from jax import lax
from jax.experimental import pallas as pl
from jax.experimental.pallas import tpu as pltpu

F32 = jnp.float32
BF16 = jnp.bfloat16

RMS_EPS = 1e-6
LN_EPS = 1e-5
HEAD_DIM = 64
V_DIM = 2 * HEAD_DIM
CHUNK = 128
GROUP_WIDTH = 128
REL_BUCKETS = 32
REL_MAX_EXACT = REL_BUCKETS // 2
REL_MAX_DIST = 128
LAM_INIT = 0.8 - 0.6 * math.exp(-0.3 * 0)
MASK_VALUE = -1e30
LOG2E = math.log2(math.e)
ONES_ROWS = 16
ATTN_HEADS_PER_STEP = 4
KEY_CHUNK = 256
EXP_LAG = 8
VALUE_LAG = 2

SEQ_TILE = 256
INPROJ_TILE = 512
INPROJ_ROWS = 512
INPROJ_COLS = 256
TAIL_TILE = 512
TAIL_ROWS = 256
TAIL_HIDDEN_CHUNK = 1536
V7X_VMEM_BYTES = 64 * 1024 * 1024
VMEM_HEADROOM = 6 * 1024 * 1024

_NT = (((1,), (1,)), ((), ()))


def _sigmoid(x):
    return 1.0 / (1.0 + jnp.exp2(x * (-LOG2E)))


def _gelu_tanh(x):
    a = -2.0 * math.sqrt(2.0 / math.pi) * LOG2E
    return x / (1.0 + jnp.exp2(x * (a + (a * 0.044715) * (x * x))))


def _vmem_limit(resident, streamed, temporaries):
    return int(min(resident + 2 * streamed + temporaries, V7X_VMEM_BYTES - VMEM_HEADROOM))


def _resident(shape):
    n = len(shape)
    return pl.BlockSpec(shape, lambda *_: (0,) * n, pipeline_mode=pl.Buffered(1))


def _ada_kernel(c_ref, w_ref, b_ref, o_ref):
    c = c_ref[...]
    ca = (c * _sigmoid(c)).astype(BF16)
    o_ref[...] = jnp.dot(ca, w_ref[...].astype(BF16), preferred_element_type=F32) + b_ref[...]


def _ada(c, w_ada, b_ada):
    B, D = c.shape
    N = w_ada.shape[1]
    tn = 1024
    return pl.pallas_call(
        _ada_kernel,
        grid=(N // tn,),
        in_specs=[pl.BlockSpec((B, D), lambda n: (0, 0)),
                  pl.BlockSpec((D, tn), lambda n: (0, n)),
                  pl.BlockSpec((1, tn), lambda n: (0, n))],
        out_specs=pl.BlockSpec((B, tn), lambda n: (0, n)),
        out_shape=jax.ShapeDtypeStruct((B, N), F32),
        name="ada",
    )(c, w_ada, b_ada.reshape(1, N))


def _bucket_tiles(T):
    kk = np.arange(T, dtype=np.int32)[:, None]
    qq = np.arange(T, dtype=np.int32)[None, :]
    out = []
    for o in (1, 0):
        dist = o * T + qq - kk
        n = np.maximum(dist, 0)
        nf = np.maximum(n, 1).astype(np.float32)
        large = REL_MAX_EXACT + (np.log(nf / np.float32(REL_MAX_EXACT))
                                 / np.float32(math.log(REL_MAX_DIST / REL_MAX_EXACT))
                                 * np.float32(REL_BUCKETS - REL_MAX_EXACT)).astype(np.int32)
        large = np.minimum(large, REL_BUCKETS - 1)
        b = np.where(n < REL_MAX_EXACT, n, large)
        out.append(np.where(dist < 0, -1, b).astype(np.int32))
    return np.concatenate(out, axis=0)


def _bias_kernel(rb_ref, bucket_ref, o_ref):
    h = pl.program_id(0)
    bk = bucket_ref[...]
    far = rb_ref[REL_BUCKETS - 1, h]
    acc = jnp.where(bk < 0, MASK_VALUE, 0.0).astype(F32)
    for b in range(REL_BUCKETS - 1):
        acc = jnp.where(bk == b, (rb_ref[b, h] - far) * LOG2E, acc)
    o_ref[...] = acc


def _bias_tiles(rel_bias, T):
    H = rel_bias.shape[1]
    buckets = jnp.asarray(_bucket_tiles(T))
    return pl.pallas_call(
        _bias_kernel,
        grid=(H,),
        in_specs=[pl.BlockSpec(memory_space=pltpu.SMEM),
                  pl.BlockSpec((2 * T, T), lambda h: (0, 0))],
        out_specs=pl.BlockSpec((None, 2 * T, T), lambda h: (h, 0, 0)),
        out_shape=jax.ShapeDtypeStruct((H, 2 * T, T), F32),
        name="bias_tiles",
    )(rel_bias, buckets)


def _inproj_kernel(x_ref, mod_ref, n1g_ref, wqT_ref, wk_ref, wvT_ref, wu_ref, wg_ref, wga_ref,
                   wgb_ref, lng_ref, lnb_ref, ws_ref, bs_ref,
                   qT_ref, k_ref, vT_ref, ob_ref, sga_ref, sgb_ref):
    tm, D = x_ref.shape
    NC = INPROJ_COLS
    T = qT_ref.shape[-1]
    chunks = [slice(c * NC, (c + 1) * NC) for c in range(D // NC)]

    rnd = [0]
    queue = []

    def put(fn, *args):
        queue.append((rnd[0], functools.partial(fn, *args)))

    R = INPROJ_ROWS

    def row_group(j):
        rs = slice(j * R, (j + 1) * R)
        g_parts, u_parts, v_parts, stats = {}, {}, {}, {}
        x = x_ref[rs, :]
        y = x * lax.rsqrt(jnp.mean(x * x, axis=-1, keepdims=True) + RMS_EPS) * n1g_ref[...]
        h = (y * (1.0 + mod_ref[1:2, :]) + mod_ref[0:1, :]).astype(BF16)

        def epi_g(c, z):
            g = _gelu_tanh(z)
            g_parts[c] = g
            rsum = jnp.sum(g, axis=-1, keepdims=True)
            stats["sum"] = rsum if c == 0 else stats["sum"] + rsum
            if c == len(chunks) - 1:
                for cc in range(len(chunks)):
                    put(ln_center, cc)
                for cc in range(len(chunks)):
                    put(ln_scale, cc)

        def ln_center(c):
            gc = g_parts[c] - stats["sum"] * (1.0 / D)
            g_parts[c] = gc
            sq = jnp.sum(gc * gc, axis=-1, keepdims=True)
            stats["sq"] = sq if c == 0 else stats["sq"] + sq

        def ln_scale(c):
            rstd = lax.rsqrt(stats["sq"] * (1.0 / D) + LN_EPS)
            v = g_parts.pop(c) * rstd * lng_ref[:, chunks[c]] + lnb_ref[:, chunks[c]]
            v_parts[c] = v.astype(BF16)

        def epi_u(c, z):
            u_parts[c] = _gelu_tanh(z)

        def epi_k(c, z):
            k_ref[rs, chunks[c]] = z.astype(BF16)

        def epi_q(c, z):
            qz = (z * (LOG2E / math.sqrt(HEAD_DIM))).astype(BF16)
            for jj in range(R // T):
                qT_ref[j * (R // T) + jj, chunks[c], :] = qz[:, jj * T:(jj + 1) * T]

        def epi_v(c, z):
            vT_ref[chunks[c], rs] = z.astype(BF16)

        def epi_gate(ref, c, z):
            ref[rs, chunks[c]] = _sigmoid(z).astype(BF16)

        chunk_rows = [slice(ci * CHUNK, (ci + 1) * CHUNK) for ci in range(R // CHUNK)]

        def group_cols(gi):
            lo = gi * GROUP_WIDTH
            return lo // NC, slice(lo % NC, lo % NC + GROUP_WIDTH)

        def epi_sv(gi, sv):
            c, local = group_cols(gi)
            cols = slice(gi * GROUP_WIDTH, (gi + 1) * GROUP_WIDTH)
            for ci, rows in enumerate(chunk_rows):
                sv_c = sv[:, ci * GROUP_WIDTH:(ci + 1) * GROUP_WIDTH]
                out_rows = slice(j * R + rows.start, j * R + rows.stop)
                ob_ref[out_rows, cols] = (u_parts[c][rows, local] * sv_c).astype(BF16)

        def rows_dot(w_ref, cols):
            return jnp.dot(h, w_ref[:, cols], preferred_element_type=F32)

        def feat_dot(wT_ref, rows):
            return lax.dot_general(wT_ref[rows, :], h, _NT, preferred_element_type=F32)

        heavy = ([(epi_g, (c,), rows_dot, (wg_ref, s)) for c, s in enumerate(chunks)]
                 + [(epi_u, (c,), rows_dot, (wu_ref, s)) for c, s in enumerate(chunks)]
                 + [(epi_gate, (sga_ref, c), rows_dot, (wga_ref, s)) for c, s in enumerate(chunks)]
                 + [(epi_gate, (sgb_ref, c), rows_dot, (wgb_ref, s)) for c, s in enumerate(chunks)])
        light = ([(epi_k, (c,), rows_dot, (wk_ref, s)) for c, s in enumerate(chunks)]
                 + [(epi_q, (c,), feat_dot, (wqT_ref, s)) for c, s in enumerate(chunks)]
                 + [(epi_v, (c,), feat_dot, (wvT_ref, s)) for c, s in enumerate(chunks)])
        order = []
        while heavy or light:
            order += heavy[:1] + light[:1]
            heavy, light = heavy[1:], light[1:]
        for epi, epi_args, dot_fn, dot_args in order:
            put(epi, *epi_args, dot_fn(*dot_args))
            yield True
        tri = (lax.broadcasted_iota(jnp.int32, (CHUNK, CHUNK), 0)
               >= lax.broadcasted_iota(jnp.int32, (CHUNK, CHUNK), 1))
        for gi in range(ws_ref.shape[0]):
            ws = jnp.where(tri, ws_ref[gi], 0.0).astype(BF16)
            c, local = group_cols(gi)
            vcat = jnp.concatenate([v_parts[c][rows, local] for rows in chunk_rows], axis=1)
            sv = jnp.dot(ws, vcat, preferred_element_type=F32) + bs_ref[:, gi:gi + 1]
            put(epi_sv, gi, sv)
            yield True

    live = [row_group(j) for j in range(tm // R)]
    while live or queue:
        live = [g for g in live if next(g, None) is not None]
        rnd[0] += 1
        ready = sum(1 for r, _ in queue if r < rnd[0] - 1) if live else len(queue)
        for _ in range(min(ready, 2 * len(live) if live else ready)):
            queue.pop(0)[1]()


def _inproj(x, mod3, norm1_g, w_in, ln_v_g, ln_v_b, w_spatial, b_spatial):
    B, S, D = x.shape
    T = SEQ_TILE
    nt = S // T
    tm = INPROJ_TILE
    wq, wk, wv, wu, wg, wga, wgb = [w_in[:, i * D:(i + 1) * D].astype(BF16) for i in range(7)]
    wqT = wq.T
    wvT = wv.T
    G = w_spatial.shape[0]
    row = lambda a: a.reshape(1, D)
    tok = pl.BlockSpec((None, tm, D), lambda b, t: (b, t, 0))
    q_featT = pl.BlockSpec((None, tm // T, D, T), lambda b, t: (b, t, 0, 0))
    v_featT = pl.BlockSpec((None, D, tm), lambda b, t: (b, 0, t))
    out_tok = jax.ShapeDtypeStruct((B, S, D), BF16)
    return pl.pallas_call(
        _inproj_kernel,
        grid=(B, S // tm),
        in_specs=[tok,
                  pl.BlockSpec((None, 6, D), lambda b, t: (b, 0, 0)),
                  _resident((1, D)),
                  _resident((D, D)), _resident((D, D)), _resident((D, D)), _resident((D, D)),
                  _resident((D, D)), _resident((D, D)), _resident((D, D)),
                  _resident((1, D)), _resident((1, D)),
                  _resident((G, CHUNK, CHUNK)), _resident((CHUNK, G))],
        out_specs=[q_featT, tok, v_featT, tok, tok, tok],
        out_shape=[jax.ShapeDtypeStruct((B, nt, D, T), BF16), out_tok,
                   jax.ShapeDtypeStruct((B, D, S), BF16), out_tok, out_tok, out_tok],
        compiler_params=pltpu.CompilerParams(
            dimension_semantics=("arbitrary", "arbitrary"),
            vmem_limit_bytes=_vmem_limit(resident=7 * D * D * 2,
                                         streamed=tm * D * 4 + 6 * tm * D * 2,
                                         temporaries=8 * tm * D * 4)),
        name="inproj",
    )(x, mod3, row(norm1_g), wqT, wk, wvT, wu, wg, wga, wgb, row(ln_v_g), row(ln_v_b),
      w_spatial, b_spatial.T)


def _attn_kernel(qT_ref, k_ref, vT_ref, tbl_ref, lam_ref, sg_ref, o_ref, va_ref):
    nt, _, T = qT_ref.shape
    n_heads = va_ref.shape[0]
    feat = lax.broadcasted_iota(jnp.int32, (V_DIM, T), 0)
    head_cols = [slice(hl * V_DIM, (hl + 1) * V_DIM) for hl in range(n_heads)]
    for hl in range(n_heads):
        va_ref[hl, 0:V_DIM, :] = vT_ref[head_cols[hl], :]
        va_ref[hl, V_DIM:, :] = jnp.ones((ONES_ROWS, va_ref.shape[2]), BF16)
    lam = (jnp.exp(jnp.sum(lam_ref[0:1, :] * lam_ref[1:2, :], axis=-1, keepdims=True))
           - jnp.exp(jnp.sum(lam_ref[2:3, :] * lam_ref[3:4, :], axis=-1, keepdims=True))
           + LAM_INIT)

    units = [(hl, i, mi) for i in reversed(range(nt)) for hl in range(n_heads) for mi in range(2)]
    tick = [0]
    s_done = {}
    p_done = {}
    outs = {}

    def key_chunks(i):
        near_lo, hi = max(i - 1, 0) * T, (i + 1) * T
        cuts = sorted(set(range(near_lo, -1, -KEY_CHUNK)) | {0})
        far = [(lo, nxt, None) for lo, nxt in zip(cuts[:-1], cuts[1:])]
        return far + [(near_lo, hi, slice(2 * T - (hi - near_lo), 2 * T))]

    def scores():
        for u in units:
            hl, i, mi = u
            in_map = (feat < HEAD_DIM) if mi == 0 else (feat >= HEAD_DIM)
            qz = jnp.where(in_map, qT_ref[i, head_cols[hl], :].astype(F32), 0.0).astype(BF16)
            for c, (lo, hi, tbl_rows) in enumerate(key_chunks(i)):
                s = jnp.dot(k_ref[lo:hi, head_cols[hl]], qz, preferred_element_type=F32)
                if tbl_rows is not None:
                    s = s + tbl_ref[hl, tbl_rows, :]
                s_done[u, c] = (s, tick[0])
                yield True

    def probs():
        for u in units:
            for c in range(len(key_chunks(u[1]))):
                while (u, c) not in s_done or tick[0] < s_done[u, c][1] + EXP_LAG:
                    yield False
                s, _ = s_done.pop((u, c))
                mc = jnp.max(s, axis=0, keepdims=True)
                p_done[u, c] = (jnp.exp2((s - mc).astype(BF16)), mc, tick[0])
                yield True

    def values():
        for u in units:
            hl, i, mi = u
            parts = []
            for c, (lo, hi, _) in enumerate(key_chunks(i)):
                while (u, c) not in p_done or tick[0] < p_done[u, c][2] + VALUE_LAG:
                    yield False
                p, mc, _ = p_done.pop((u, c))
                parts.append((jnp.dot(va_ref[hl, :, lo:hi], p, preferred_element_type=F32), mc))
                yield True
            m = functools.reduce(jnp.maximum, [mc for _, mc in parts])
            acc = functools.reduce(lambda a, b: a + b, [d * jnp.exp2(mc - m) for d, mc in parts])
            outs[u] = acc[0:V_DIM, :] * (1.0 / acc[V_DIM:V_DIM + 1, :])
            if mi == 1:
                oT = outs.pop((hl, i, 0)) - lam * outs.pop((hl, i, 1))
                yT = oT * lax.rsqrt(jnp.mean(oT * oT, axis=0, keepdims=True) + RMS_EPS)
                y = yT.T * sg_ref[...] * (1.0 - LAM_INIT)
                o_ref[i * T:(i + 1) * T, head_cols[hl]] = y.astype(BF16)

    streams = [scores(), probs(), values()]
    while streams:
        streams = [g for g in streams if next(g, None) is not None]
        tick[0] += 1


def _attention(qT, k, vT, tbl, lam_vecs, subln_g):
    B, nt, D, T = qT.shape
    S = nt * T
    H = D // V_DIM
    hp = ATTN_HEADS_PER_STEP
    W = hp * V_DIM
    return pl.pallas_call(
        _attn_kernel,
        grid=(B, H // hp),
        in_specs=[pl.BlockSpec((None, nt, W, T), lambda b, h: (b, 0, h, 0)),
                  pl.BlockSpec((None, S, W), lambda b, h: (b, 0, h)),
                  pl.BlockSpec((None, W, S), lambda b, h: (b, h, 0)),
                  pl.BlockSpec((hp, 2 * T, T), lambda b, h: (h, 0, 0)),
                  pl.BlockSpec((4, HEAD_DIM), lambda b, h: (0, 0)),
                  pl.BlockSpec((1, V_DIM), lambda b, h: (0, 0))],
        out_specs=pl.BlockSpec((None, S, W), lambda b, h: (b, 0, h)),
        out_shape=jax.ShapeDtypeStruct((B, S, D), BF16),
        scratch_shapes=[pltpu.VMEM((hp, V_DIM + ONES_ROWS, S), BF16)],
        compiler_params=pltpu.CompilerParams(
            dimension_semantics=("arbitrary", "arbitrary"),
            vmem_limit_bytes=_vmem_limit(resident=hp * (V_DIM + ONES_ROWS) * S * 2,
                                         streamed=4 * S * W * 2 + hp * 2 * T * T * 4,
                                         temporaries=(EXP_LAG + VALUE_LAG + 8) * 2 * T * T * 4)),
        name="attn",
    )(qT, k, vT, tbl, lam_vecs, subln_g.reshape(1, V_DIM))


def _tail_kernel(x_ref, oa_ref, ob_ref, sga_ref, sgb_ref, mod_ref, n2g_ref, fg_ref,
                 wpa_ref, wpb_ref, wo_ref, wfi_ref, wfo_ref, o_ref):
    hidden = wfo_ref.shape[0]
    tm = x_ref.shape[0]
    bounds = list(range(0, hidden, TAIL_HIDDEN_CHUNK)) + [hidden]

    def row_group(rows):
        pa = jnp.dot(oa_ref[rows, :], wpa_ref[...], preferred_element_type=F32)
        pb = jnp.dot(ob_ref[rows, :], wpb_ref[...], preferred_element_type=F32)
        yield True
        merged = (sga_ref[rows, :].astype(F32) * pa + sgb_ref[rows, :].astype(F32) * pb)
        x1 = x_ref[rows, :] + mod_ref[2:3, :] * jnp.dot(merged.astype(BF16), wo_ref[...],
                                                        preferred_element_type=F32)
        yield True
        y = x1 * lax.rsqrt(jnp.mean(x1 * x1, axis=-1, keepdims=True) + RMS_EPS) * n2g_ref[...]
        h2 = (y * (1.0 + mod_ref[4:5, :]) + mod_ref[3:4, :]).astype(BF16)
        ffn = None
        for lo, hi in zip(bounds[:-1], bounds[1:]):
            gate = jnp.dot(h2, wfi_ref[:, lo:hi], preferred_element_type=F32)
            up = jnp.dot(h2, wfi_ref[:, hidden + lo:hidden + hi], preferred_element_type=F32)
            yield True
            act = (gate * _sigmoid(gate) * up).astype(BF16)
            part = jnp.dot(act, wfo_ref[lo:hi, :], preferred_element_type=F32)
            ffn = part if ffn is None else ffn + part
        yield True
        x2 = x1 + mod_ref[5:6, :] * ffn
        o_ref[rows, :] = (x2 * lax.rsqrt(jnp.mean(x2 * x2, axis=-1, keepdims=True) + RMS_EPS)
                          * fg_ref[...])

    live = [row_group(slice(r, r + TAIL_ROWS)) for r in range(0, tm, TAIL_ROWS)]
    while live:
        live = [g for g in live if next(g, None) is not None]


def _tail(x, oa, ob, sga, sgb, mod3, norm2_g, final_g, w_proj_a, w_proj_b, w_out, w_ffn_in,
          w_ffn_out):
    B, S, D = x.shape
    T = TAIL_TILE
    hidden = w_ffn_out.shape[0]
    row = lambda a: a.reshape(1, D)
    tok = pl.BlockSpec((None, T, D), lambda b, t: (b, t, 0))
    return pl.pallas_call(
        _tail_kernel,
        grid=(B, S // T),
        in_specs=[tok, tok, tok, tok, tok,
                  pl.BlockSpec((None, 6, D), lambda b, t: (b, 0, 0)),
                  _resident((1, D)), _resident((1, D)),
                  _resident((D, D)), _resident((D, D)), _resident((D, D)),
                  _resident((D, 2 * hidden)), _resident((hidden, D))],
        out_specs=tok,
        out_shape=jax.ShapeDtypeStruct((B, S, D), F32),
        compiler_params=pltpu.CompilerParams(
            dimension_semantics=("arbitrary", "arbitrary"),
            vmem_limit_bytes=_vmem_limit(
                resident=(3 * D * D + 3 * D * hidden) * 2,
                streamed=2 * T * D * 4 + 4 * T * D * 2,
                temporaries=(T // TAIL_ROWS) * TAIL_ROWS * (6 * D + 2 * TAIL_HIDDEN_CHUNK) * 4)),
        name="tail",
    )(x, oa, ob, sga, sgb, mod3, row(norm2_g), row(final_g),
      w_proj_a.astype(BF16), w_proj_b.astype(BF16), w_out.astype(BF16),
      w_ffn_in.astype(BF16), w_ffn_out.astype(BF16))


def kernel(x, c, w_ada, b_ada, norm1_g, norm2_g, w_in, lambda_q1, lambda_k1, lambda_q2, lambda_k2,
           subln_g, ln_v_g, ln_v_b, w_spatial, b_spatial, w_proj_a, w_proj_b, w_out, w_ffn_in,
           w_ffn_out, rel_bias, final_g):
    B, S, D = x.shape
    assert w_ada.shape[0] == 1, "single-layer block"
    assert S % INPROJ_TILE == 0 and S % TAIL_TILE == 0
    assert INPROJ_TILE % INPROJ_ROWS == 0 and INPROJ_ROWS % SEQ_TILE == 0
    assert INPROJ_ROWS % CHUNK == 0 and TAIL_TILE % TAIL_ROWS == 0
    assert SEQ_TILE >= REL_MAX_DIST, "bias must be constant beyond the first off-diagonal tile"

    mod3 = _ada(c, w_ada[0], b_ada[0]).reshape(B, 6, D)
    tbl = _bias_tiles(rel_bias, SEQ_TILE)
    qT, k, vT, ob, sga, sgb = _inproj(x, mod3, norm1_g[0], w_in[0], ln_v_g[0], ln_v_b[0],
                                      w_spatial[0], b_spatial[0])
    lam_vecs = jnp.concatenate([lambda_q1, lambda_k1, lambda_q2, lambda_k2], axis=0)
    oa = _attention(qT, k, vT, tbl, lam_vecs, subln_g[0])
    return _tail(x, oa, ob, sga, sgb, mod3, norm2_g[0], final_g, w_proj_a[0], w_proj_b[0],
                 w_out[0], w_ffn_in[0], w_ffn_out[0])
```

```python
import functools
import math

import numpy as np
import jax
import jax.numpy as jnp
from jax import lax
from jax.experimental import pallas as pl
from jax.experimental.pallas import tpu as pltpu

F32 = jnp.float32
BF16 = jnp.bfloat16

RMS_EPS = 1e-6
LN_EPS = 1e-5
HEAD_DIM = 64
V_DIM = 2 * HEAD_DIM
CHUNK = 128
GROUP_WIDTH = 128
REL_BUCKETS = 32
REL_MAX_EXACT = REL_BUCKETS // 2
REL_MAX_DIST = 128
LAM_INIT = 0.8 - 0.6 * math.exp(-0.3 * 0)
MASK_VALUE = -1e30
LOG2E = math.log2(math.e)
ONES_ROWS = 16
ATTN_HEADS_PER_STEP = 2
KEY_CHUNK = 256
EXP_LAG = 8
VALUE_LAG = 2

SEQ_TILE = 256
INPROJ_TILE = 512
INPROJ_ROWS = 512
INPROJ_COLS = 256
TAIL_TILE = 512
TAIL_ROWS = 256
TAIL_HIDDEN_CHUNK = 1536
V7X_VMEM_BYTES = 64 * 1024 * 1024
VMEM_HEADROOM = 6 * 1024 * 1024

_NT = (((1,), (1,)), ((), ()))
SEC_Q, SEC_K, SEC_V, SEC_U, SEC_G, SEC_GA, SEC_GB = range(7)


def _sigmoid(x):
    return 1.0 / (1.0 + jnp.exp2(x * (-LOG2E)))


def _gelu_tanh(x):
    a = -2.0 * math.sqrt(2.0 / math.pi) * LOG2E
    return x / (1.0 + jnp.exp2(x * (a + (a * 0.044715) * (x * x))))


def _vmem_limit(resident, streamed, temporaries):
    return int(min(resident + 2 * streamed + temporaries, V7X_VMEM_BYTES - VMEM_HEADROOM))


def _resident(shape):
    n = len(shape)
    return pl.BlockSpec(shape, lambda *_: (0,) * n, pipeline_mode=pl.Buffered(1))


def _ada_kernel(c_ref, w_ref, b_ref, o_ref):
    c = c_ref[...]
    ca = (c * _sigmoid(c)).astype(BF16)
    o_ref[...] = jnp.dot(ca, w_ref[...].astype(BF16), preferred_element_type=F32) + b_ref[...]


def _ada(c, w_ada, b_ada):
    B, D = c.shape
    N = w_ada.shape[1]
    tn = 1024
    return pl.pallas_call(
        _ada_kernel,
        grid=(N // tn,),
        in_specs=[pl.BlockSpec((B, D), lambda n: (0, 0)),
                  pl.BlockSpec((D, tn), lambda n: (0, n)),
                  pl.BlockSpec((1, tn), lambda n: (0, n))],
        out_specs=pl.BlockSpec((B, tn), lambda n: (0, n)),
        out_shape=jax.ShapeDtypeStruct((B, N), F32),
        name="ada",
    )(c, w_ada, b_ada.reshape(1, N))


def _bucket_tiles(T):
    kk = np.arange(T, dtype=np.int32)[:, None]
    qq = np.arange(T, dtype=np.int32)[None, :]
    out = []
    for o in (1, 0):
        dist = o * T + qq - kk
        n = np.maximum(dist, 0)
        nf = np.maximum(n, 1).astype(np.float32)
        large = REL_MAX_EXACT + (np.log(nf / np.float32(REL_MAX_EXACT))
                                 / np.float32(math.log(REL_MAX_DIST / REL_MAX_EXACT))
                                 * np.float32(REL_BUCKETS - REL_MAX_EXACT)).astype(np.int32)
        large = np.minimum(large, REL_BUCKETS - 1)
        b = np.where(n < REL_MAX_EXACT, n, large)
        out.append(np.where(dist < 0, -1, b).astype(np.int32))
    return np.concatenate(out, axis=0)


def _bias_kernel(rb_ref, bucket_ref, o_ref):
    h = pl.program_id(0)
    bk = bucket_ref[...]
    far = rb_ref[REL_BUCKETS - 1, h]
    acc = jnp.where(bk < 0, MASK_VALUE, 0.0).astype(F32)
    for b in range(REL_BUCKETS - 1):
        acc = jnp.where(bk == b, (rb_ref[b, h] - far) * LOG2E, acc)
    o_ref[...] = acc


def _bias_tiles(rel_bias, T):
    H = rel_bias.shape[1]
    buckets = jnp.asarray(_bucket_tiles(T))
    return pl.pallas_call(
        _bias_kernel,
        grid=(H,),
        in_specs=[pl.BlockSpec(memory_space=pltpu.SMEM),
                  pl.BlockSpec((2 * T, T), lambda h: (0, 0))],
        out_specs=pl.BlockSpec((None, 2 * T, T), lambda h: (h, 0, 0)),
        out_shape=jax.ShapeDtypeStruct((H, 2 * T, T), F32),
        name="bias_tiles",
    )(rel_bias, buckets)


def _inproj_kernel(x_ref, mod_ref, n1g_ref, wqT_ref, wvT_ref, w_ref, lng_ref, lnb_ref, ws_ref, bs_ref,
                   qT_ref, k_ref, vT_ref, ob_ref, sga_ref, sgb_ref):
    tm, D = x_ref.shape
    NC = INPROJ_COLS
    T = qT_ref.shape[-1]
    chunks = [slice(c * NC, (c + 1) * NC) for c in range(D // NC)]

    rnd = [0]
    queue = []

    def put(fn, *args):
        queue.append((rnd[0], functools.partial(fn, *args)))

    R = INPROJ_ROWS

    def row_group(j):
        rs = slice(j * R, (j + 1) * R)
        g_parts, u_parts, v_parts, stats = {}, {}, {}, {}
        x = x_ref[rs, :]
        y = x * lax.rsqrt(jnp.mean(x * x, axis=-1, keepdims=True) + RMS_EPS) * n1g_ref[...]
        h = (y * (1.0 + mod_ref[1:2, :]) + mod_ref[0:1, :]).astype(BF16)

        def epi_g(c, z):
            g = _gelu_tanh(z)
            g_parts[c] = g
            rsum = jnp.sum(g, axis=-1, keepdims=True)
            stats["sum"] = rsum if c == 0 else stats["sum"] + rsum
            if c == len(chunks) - 1:
                for cc in range(len(chunks)):
                    put(ln_center, cc)
                for cc in range(len(chunks)):
                    put(ln_scale, cc)

        def ln_center(c):
            gc = g_parts[c] - stats["sum"] * (1.0 / D)
            g_parts[c] = gc
            sq = jnp.sum(gc * gc, axis=-1, keepdims=True)
            stats["sq"] = sq if c == 0 else stats["sq"] + sq

        def ln_scale(c):
            rstd = lax.rsqrt(stats["sq"] * (1.0 / D) + LN_EPS)
            v = g_parts.pop(c) * rstd * lng_ref[:, chunks[c]] + lnb_ref[:, chunks[c]]
            v_parts[c] = v.astype(BF16)

        def epi_u(c, z):
            u_parts[c] = _gelu_tanh(z)

        def epi_k(c, z):
            k_ref[rs, chunks[c]] = z.astype(BF16)

        def epi_q(c, z):
            qz = (z * (LOG2E / math.sqrt(HEAD_DIM))).astype(BF16)
            for jj in range(R // T):
                qT_ref[j * (R // T) + jj, chunks[c], :] = qz[:, jj * T:(jj + 1) * T]

        def epi_v(c, z):
            vT_ref[chunks[c], rs] = z.astype(BF16)

        def epi_gate(ref, c, z):
            ref[rs, chunks[c]] = _sigmoid(z).astype(BF16)

        chunk_rows = [slice(ci * CHUNK, (ci + 1) * CHUNK) for ci in range(R // CHUNK)]

        def group_cols(gi):
            lo = gi * GROUP_WIDTH
            return lo // NC, slice(lo % NC, lo % NC + GROUP_WIDTH)

        def epi_sv(gi, sv):
            c, local = group_cols(gi)
            cols = slice(gi * GROUP_WIDTH, (gi + 1) * GROUP_WIDTH)
            for ci, rows in enumerate(chunk_rows):
                sv_c = sv[:, ci * GROUP_WIDTH:(ci + 1) * GROUP_WIDTH]
                out_rows = slice(j * R + rows.start, j * R + rows.stop)
                ob_ref[out_rows, cols] = (u_parts[c][rows, local] * sv_c).astype(BF16)

        def rows_dot(sec, cols):
            return jnp.dot(h, w_ref[:, sec * D + cols.start:sec * D + cols.stop],
                           preferred_element_type=F32)

        def feat_dot(wT_ref, rows):
            return lax.dot_general(wT_ref[rows, :], h, _NT, preferred_element_type=F32)

        heavy = ([(epi_g, (c,), rows_dot, (SEC_G, s)) for c, s in enumerate(chunks)]
                 + [(epi_u, (c,), rows_dot, (SEC_U, s)) for c, s in enumerate(chunks)]
                 + [(epi_gate, (sga_ref, c), rows_dot, (SEC_GA, s)) for c, s in enumerate(chunks)]
                 + [(epi_gate, (sgb_ref, c), rows_dot, (SEC_GB, s)) for c, s in enumerate(chunks)])
        light = ([(epi_k, (c,), rows_dot, (SEC_K, s)) for c, s in enumerate(chunks)]
                 + [(epi_q, (c,), feat_dot, (wqT_ref, s)) for c, s in enumerate(chunks)]
                 + [(epi_v, (c,), feat_dot, (wvT_ref, s)) for c, s in enumerate(chunks)])
        order = []
        while heavy or light:
            order += heavy[:1] + light[:1]
            heavy, light = heavy[1:], light[1:]
        tri = (lax.broadcasted_iota(jnp.int32, (CHUNK, CHUNK), 0)
               >= lax.broadcasted_iota(jnp.int32, (CHUNK, CHUNK), 1))

        def spatial(gi):
            ws = jnp.where(tri, ws_ref[gi], 0.0).astype(BF16)
            c, local = group_cols(gi)
            vcat = jnp.concatenate([v_parts[c][rows, local] for rows in chunk_rows], axis=1)
            sv = jnp.dot(ws, vcat, preferred_element_type=F32) + bs_ref[:, gi:gi + 1]
            put(epi_sv, gi, sv)

        spatial_todo = list(range(ws_ref.shape[0]))
        for epi, epi_args, dot_fn, dot_args in order:
            put(epi, *epi_args, dot_fn(*dot_args))
            yield True
            if spatial_todo and len(v_parts) == len(chunks) and len(u_parts) == len(chunks):
                spatial(spatial_todo.pop(0))
                yield True
        for gi in spatial_todo:
            spatial(gi)
            yield True

    live = [row_group(j) for j in range(tm // R)]
    while live or queue:
        live = [g for g in live if next(g, None) is not None]
        rnd[0] += 1
        ready = sum(1 for r, _ in queue if r < rnd[0] - 1) if live else len(queue)
        for _ in range(min(ready, 2 * len(live) if live else ready)):
            queue.pop(0)[1]()


def _inproj(x, mod3, norm1_g, w_in, ln_v_g, ln_v_b, w_spatial, b_spatial):
    B, S, D = x.shape
    T = SEQ_TILE
    nt = S // T
    tm = INPROJ_TILE
    w_bf = w_in.astype(BF16)
    wqT = w_bf[:, SEC_Q * D:(SEC_Q + 1) * D].T
    wvT = w_bf[:, SEC_V * D:(SEC_V + 1) * D].T
    G = w_spatial.shape[0]
    row = lambda a: a.reshape(1, D)
    tok = pl.BlockSpec((None, tm, D), lambda b, t: (b, t, 0))
    q_featT = pl.BlockSpec((None, tm // T, D, T), lambda b, t: (b, t, 0, 0))
    v_featT = pl.BlockSpec((None, D, tm), lambda b, t: (b, 0, t))
    out_tok = jax.ShapeDtypeStruct((B, S, D), BF16)
    return pl.pallas_call(
        _inproj_kernel,
        grid=(B, S // tm),
        in_specs=[tok,
                  pl.BlockSpec((None, 6, D), lambda b, t: (b, 0, 0)),
                  _resident((1, D)),
                  _resident((D, D)), _resident((D, D)), _resident(w_bf.shape),
                  _resident((1, D)), _resident((1, D)),
                  _resident((G, CHUNK, CHUNK)), _resident((CHUNK, G))],
        out_specs=[q_featT, tok, v_featT, tok, tok, tok],
        out_shape=[jax.ShapeDtypeStruct((B, nt, D, T), BF16), out_tok,
                   jax.ShapeDtypeStruct((B, D, S), BF16), out_tok, out_tok, out_tok],
        compiler_params=pltpu.CompilerParams(
            dimension_semantics=("arbitrary", "arbitrary"),
            vmem_limit_bytes=_vmem_limit(resident=(2 * D * D + w_bf.size) * 2,
                                         streamed=tm * D * 4 + 6 * tm * D * 2,
                                         temporaries=8 * tm * D * 4)),
        name="inproj",
    )(x, mod3, row(norm1_g), wqT, wvT, w_bf, row(ln_v_g), row(ln_v_b),
      w_spatial, b_spatial.T)


def _attn_kernel(qT_ref, k_ref, vT_ref, tbl_ref, lam_ref, sg_ref, o_ref, va_ref):
    nt, _, T = qT_ref.shape
    n_heads = va_ref.shape[0]
    feat = lax.broadcasted_iota(jnp.int32, (V_DIM, T), 0)
    head_cols = [slice(hl * V_DIM, (hl + 1) * V_DIM) for hl in range(n_heads)]
    for hl in range(n_heads):
        va_ref[hl, 0:V_DIM, :] = vT_ref[head_cols[hl], :]
        va_ref[hl, V_DIM:, :] = jnp.ones((ONES_ROWS, va_ref.shape[2]), BF16)
    lam = (jnp.exp(jnp.sum(lam_ref[0:1, :] * lam_ref[1:2, :], axis=-1, keepdims=True))
           - jnp.exp(jnp.sum(lam_ref[2:3, :] * lam_ref[3:4, :], axis=-1, keepdims=True))
           + LAM_INIT)

    units = [(hl, i, mi) for i in reversed(range(nt)) for hl in range(n_heads) for mi in range(2)]
    tick = [0]
    s_done = {}
    p_done = {}
    outs = {}

    def key_chunks(i):
        near_lo, hi = max(i - 1, 0) * T, (i + 1) * T
        cuts = sorted(set(range(near_lo, -1, -KEY_CHUNK)) | {0})
        far = [(lo, nxt, None) for lo, nxt in zip(cuts[:-1], cuts[1:])]
        return far + [(near_lo, hi, slice(2 * T - (hi - near_lo), 2 * T))]

    def scores():
        for u in units:
            hl, i, mi = u
            in_map = (feat < HEAD_DIM) if mi == 0 else (feat >= HEAD_DIM)
            qz = jnp.where(in_map, qT_ref[i, head_cols[hl], :].astype(F32), 0.0).astype(BF16)
            for c, (lo, hi, tbl_rows) in enumerate(key_chunks(i)):
                s = jnp.dot(k_ref[lo:hi, head_cols[hl]], qz, preferred_element_type=F32)
                if tbl_rows is not None:
                    s = s + tbl_ref[hl, tbl_rows, :]
                s_done[u, c] = (s, tick[0])
                yield True

    def probs():
        for u in units:
            for c in range(len(key_chunks(u[1]))):
                while (u, c) not in s_done or tick[0] < s_done[u, c][1] + EXP_LAG:
                    yield False
                s, _ = s_done.pop((u, c))
                mc = jnp.max(s, axis=0, keepdims=True)
                p_done[u, c] = (jnp.exp2((s - mc).astype(BF16)), mc, tick[0])
                yield True

    def values():
        for u in units:
            hl, i, mi = u
            parts = []
            for c, (lo, hi, _) in enumerate(key_chunks(i)):
                while (u, c) not in p_done or tick[0] < p_done[u, c][2] + VALUE_LAG:
                    yield False
                p, mc, _ = p_done.pop((u, c))
                parts.append((jnp.dot(va_ref[hl, :, lo:hi], p, preferred_element_type=F32), mc))
                yield True
            m = functools.reduce(jnp.maximum, [mc for _, mc in parts])
            acc = functools.reduce(lambda a, b: a + b, [d * jnp.exp2(mc - m) for d, mc in parts])
            outs[u] = acc[0:V_DIM, :] * (1.0 / acc[V_DIM:V_DIM + 1, :])
            if mi == 1:
                oT = outs.pop((hl, i, 0)) - lam * outs.pop((hl, i, 1))
                yT = oT * lax.rsqrt(jnp.mean(oT * oT, axis=0, keepdims=True) + RMS_EPS)
                y = yT.T * sg_ref[...] * (1.0 - LAM_INIT)
                o_ref[i * T:(i + 1) * T, head_cols[hl]] = y.astype(BF16)

    streams = [scores(), probs(), values()]
    while streams:
        streams = [g for g in streams if next(g, None) is not None]
        tick[0] += 1


def _attention(qT, k, vT, tbl, lam_vecs, subln_g):
    B, nt, D, T = qT.shape
    S = nt * T
    H = D // V_DIM
    hp = ATTN_HEADS_PER_STEP
    W = hp * V_DIM
    return pl.pallas_call(
        _attn_kernel,
        grid=(B, H // hp),
        in_specs=[pl.BlockSpec((None, nt, W, T), lambda b, h: (b, 0, h, 0)),
                  pl.BlockSpec((None, S, W), lambda b, h: (b, 0, h)),
                  pl.BlockSpec((None, W, S), lambda b, h: (b, h, 0)),
                  pl.BlockSpec((hp, 2 * T, T), lambda b, h: (h, 0, 0)),
                  pl.BlockSpec((4, HEAD_DIM), lambda b, h: (0, 0)),
                  pl.BlockSpec((1, V_DIM), lambda b, h: (0, 0))],
        out_specs=pl.BlockSpec((None, S, W), lambda b, h: (b, 0, h)),
        out_shape=jax.ShapeDtypeStruct((B, S, D), BF16),
        scratch_shapes=[pltpu.VMEM((hp, V_DIM + ONES_ROWS, S), BF16)],
        compiler_params=pltpu.CompilerParams(
            dimension_semantics=("arbitrary", "arbitrary"),
            vmem_limit_bytes=_vmem_limit(resident=hp * (V_DIM + ONES_ROWS) * S * 2,
                                         streamed=4 * S * W * 2 + hp * 2 * T * T * 4,
                                         temporaries=(EXP_LAG + VALUE_LAG + 8) * 2 * T * T * 4)),
        name="attn",
    )(qT, k, vT, tbl, lam_vecs, subln_g.reshape(1, V_DIM))


def _tail_kernel(x_ref, oa_ref, ob_ref, sga_ref, sgb_ref, mod_ref, n2g_ref, fg_ref,
                 wpa_ref, wpb_ref, wo_ref, wfi_ref, wfo_ref, o_ref):
    hidden = wfo_ref.shape[0]
    tm = x_ref.shape[0]
    bounds = list(range(0, hidden, TAIL_HIDDEN_CHUNK)) + [hidden]

    def row_group(rows):
        pa = jnp.dot(oa_ref[rows, :], wpa_ref[...], preferred_element_type=F32)
        pb = jnp.dot(ob_ref[rows, :], wpb_ref[...], preferred_element_type=F32)
        yield True
        merged = (sga_ref[rows, :].astype(F32) * pa + sgb_ref[rows, :].astype(F32) * pb)
        x1 = x_ref[rows, :] + mod_ref[2:3, :] * jnp.dot(merged.astype(BF16), wo_ref[...],
                                                        preferred_element_type=F32)
        yield True
        y = x1 * lax.rsqrt(jnp.mean(x1 * x1, axis=-1, keepdims=True) + RMS_EPS) * n2g_ref[...]
        h2 = (y * (1.0 + mod_ref[4:5, :]) + mod_ref[3:4, :]).astype(BF16)
        ffn = None
        for lo, hi in zip(bounds[:-1], bounds[1:]):
            gate = jnp.dot(h2, wfi_ref[:, lo:hi], preferred_element_type=F32)
            up = jnp.dot(h2, wfi_ref[:, hidden + lo:hidden + hi], preferred_element_type=F32)
            yield True
            act = (gate * _sigmoid(gate) * up).astype(BF16)
            part = jnp.dot(act, wfo_ref[lo:hi, :], preferred_element_type=F32)
            ffn = part if ffn is None else ffn + part
        yield True
        x2 = x1 + mod_ref[5:6, :] * ffn
        o_ref[rows, :] = (x2 * lax.rsqrt(jnp.mean(x2 * x2, axis=-1, keepdims=True) + RMS_EPS)
                          * fg_ref[...])

    live = [row_group(slice(r, r + TAIL_ROWS)) for r in range(0, tm, TAIL_ROWS)]
    while live:
        live = [g for g in live if next(g, None) is not None]


def _tail(x, oa, ob, sga, sgb, mod3, norm2_g, final_g, w_proj_a, w_proj_b, w_out, w_ffn_in,
          w_ffn_out):
    B, S, D = x.shape
    T = TAIL_TILE
    hidden = w_ffn_out.shape[0]
    row = lambda a: a.reshape(1, D)
    tok = pl.BlockSpec((None, T, D), lambda b, t: (b, t, 0))
    return pl.pallas_call(
        _tail_kernel,
        grid=(B, S // T),
        in_specs=[tok, tok, tok, tok, tok,
                  pl.BlockSpec((None, 6, D), lambda b, t: (b, 0, 0)),
                  _resident((1, D)), _resident((1, D)),
                  _resident((D, D)), _resident((D, D)), _resident((D, D)),
                  _resident((D, 2 * hidden)), _resident((hidden, D))],
        out_specs=tok,
        out_shape=jax.ShapeDtypeStruct((B, S, D), F32),
        compiler_params=pltpu.CompilerParams(
            dimension_semantics=("arbitrary", "arbitrary"),
            vmem_limit_bytes=_vmem_limit(
                resident=(3 * D * D + 3 * D * hidden) * 2,
                streamed=2 * T * D * 4 + 4 * T * D * 2,
                temporaries=(T // TAIL_ROWS) * TAIL_ROWS * (6 * D + 2 * TAIL_HIDDEN_CHUNK) * 4)),
        name="tail",
    )(x, oa, ob, sga, sgb, mod3, row(norm2_g), row(final_g),
      w_proj_a.astype(BF16), w_proj_b.astype(BF16), w_out.astype(BF16),
      w_ffn_in.astype(BF16), w_ffn_out.astype(BF16))


def kernel(x, c, w_ada, b_ada, norm1_g, norm2_g, w_in, lambda_q1, lambda_k1, lambda_q2, lambda_k2,
           subln_g, ln_v_g, ln_v_b, w_spatial, b_spatial, w_proj_a, w_proj_b, w_out, w_ffn_in,
           w_ffn_out, rel_bias, final_g):
    B, S, D = x.shape
    assert w_ada.shape[0] == 1, "single-layer block"
    assert S % INPROJ_TILE == 0 and S % TAIL_TILE == 0
    assert INPROJ_TILE % INPROJ_ROWS == 0 and INPROJ_ROWS % SEQ_TILE == 0
    assert INPROJ_ROWS % CHUNK == 0 and TAIL_TILE % TAIL_ROWS == 0
    assert SEQ_TILE >= REL_MAX_DIST, "bias must be constant beyond the first off-diagonal tile"

    mod3 = _ada(c, w_ada[0], b_ada[0]).reshape(B, 6, D)
    tbl = _bias_tiles(rel_bias, SEQ_TILE)
    qT, k, vT, ob, sga, sgb = _inproj(x, mod3, norm1_g[0], w_in[0], ln_v_g[0], ln_v_b[0],
                                      w_spatial[0], b_spatial[0])
    lam_vecs = jnp.concatenate([lambda_q1, lambda_k1, lambda_q2, lambda_k2], axis=0)
    oa = _attention(qT, k, vT, tbl, lam_vecs, subln_g[0])
    return _tail(x, oa, ob, sga, sgb, mod3, norm2_g[0], final_g, w_proj_a[0], w_proj_b[0],
                 w_out[0], w_ffn_in[0], w_ffn_out[0])
```

```python
import functools
import math

import numpy as np
import jax
import jax.numpy as jnp
from jax import lax
from jax.experimental import pallas as pl
from jax.experimental.pallas import tpu as pltpu

F32 = jnp.float32
BF16 = jnp.bfloat16

RMS_EPS = 1e-6
LN_EPS = 1e-5
HEAD_DIM = 64
V_DIM = 2 * HEAD_DIM
CHUNK = 128
GROUP_WIDTH = 128
REL_BUCKETS = 32
REL_MAX_EXACT = REL_BUCKETS // 2
REL_MAX_DIST = 128
LAM_INIT = 0.8 - 0.6 * math.exp(-0.3 * 0)
MASK_VALUE = -1e30
LOG2E = math.log2(math.e)
ONES_ROWS = 16
ATTN_HEADS_PER_STEP = 2
KEY_CHUNK = 256
EXP_LAG = 8
VALUE_LAG = 2

SEQ_TILE = 256
INPROJ_TILE = 512
INPROJ_ROWS = 512
INPROJ_COLS = 256
TAIL_TILE = 512
TAIL_ROWS = 256
TAIL_HIDDEN_CHUNK = 1536
V7X_VMEM_BYTES = 64 * 1024 * 1024
VMEM_HEADROOM = 6 * 1024 * 1024
ATTN_TEMP_BYTES = 36 * 1024 * 1024

_NT = (((1,), (1,)), ((), ()))
SEC_Q, SEC_K, SEC_V, SEC_U, SEC_G, SEC_GA, SEC_GB = range(7)


def _sigmoid(x):
    return 1.0 / (1.0 + jnp.exp2(x * (-LOG2E)))


def _gelu_tanh(x):
    a = -2.0 * math.sqrt(2.0 / math.pi) * LOG2E
    return x / (1.0 + jnp.exp2(x * (a + (a * 0.044715) * (x * x))))


def _vmem_limit(resident, streamed, temporaries):
    return int(min(resident + 2 * streamed + temporaries, V7X_VMEM_BYTES - VMEM_HEADROOM))


def _resident(shape):
    n = len(shape)
    return pl.BlockSpec(shape, lambda *_: (0,) * n, pipeline_mode=pl.Buffered(1))


def _ada_kernel(c_ref, w_ref, b_ref, o_ref):
    c = c_ref[...]
    ca = (c * _sigmoid(c)).astype(BF16)
    o_ref[...] = jnp.dot(ca, w_ref[...].astype(BF16), preferred_element_type=F32) + b_ref[...]


def _ada(c, w_ada, b_ada):
    B, D = c.shape
    N = w_ada.shape[1]
    tn = 1024
    return pl.pallas_call(
        _ada_kernel,
        grid=(N // tn,),
        in_specs=[pl.BlockSpec((B, D), lambda n: (0, 0)),
                  pl.BlockSpec((D, tn), lambda n: (0, n)),
                  pl.BlockSpec((1, tn), lambda n: (0, n))],
        out_specs=pl.BlockSpec((B, tn), lambda n: (0, n)),
        out_shape=jax.ShapeDtypeStruct((B, N), F32),
        name="ada",
    )(c, w_ada, b_ada.reshape(1, N))


def _bucket_tiles(T):
    kk = np.arange(T, dtype=np.int32)[:, None]
    qq = np.arange(T, dtype=np.int32)[None, :]
    out = []
    for o in (1, 0):
        dist = o * T + qq - kk
        n = np.maximum(dist, 0)
        nf = np.maximum(n, 1).astype(np.float32)
        large = REL_MAX_EXACT + (np.log(nf / np.float32(REL_MAX_EXACT))
                                 / np.float32(math.log(REL_MAX_DIST / REL_MAX_EXACT))
                                 * np.float32(REL_BUCKETS - REL_MAX_EXACT)).astype(np.int32)
        large = np.minimum(large, REL_BUCKETS - 1)
        b = np.where(n < REL_MAX_EXACT, n, large)
        out.append(np.where(dist < 0, -1, b).astype(np.int32))
    return np.concatenate(out, axis=0)


def _bias_kernel(rb_ref, bucket_ref, o_ref):
    h = pl.program_id(0)
    bk = bucket_ref[...]
    far = rb_ref[REL_BUCKETS - 1, h]
    acc = jnp.where(bk < 0, MASK_VALUE, 0.0).astype(F32)
    for b in range(REL_BUCKETS - 1):
        acc = jnp.where(bk == b, (rb_ref[b, h] - far) * LOG2E, acc)
    o_ref[...] = acc


def _bias_tiles(rel_bias, T):
    H = rel_bias.shape[1]
    buckets = jnp.asarray(_bucket_tiles(T))
    return pl.pallas_call(
        _bias_kernel,
        grid=(H,),
        in_specs=[pl.BlockSpec(memory_space=pltpu.SMEM),
                  pl.BlockSpec((2 * T, T), lambda h: (0, 0))],
        out_specs=pl.BlockSpec((None, 2 * T, T), lambda h: (h, 0, 0)),
        out_shape=jax.ShapeDtypeStruct((H, 2 * T, T), F32),
        name="bias_tiles",
    )(rel_bias, buckets)


def _inproj_kernel(x_ref, mod_ref, n1g_ref, wqT_ref, wvT_ref, w_ref, lng_ref, lnb_ref, ws_ref, bs_ref,
                   qT_ref, k_ref, vT_ref, ob_ref, sga_ref, sgb_ref):
    tm, D = x_ref.shape
    NC = INPROJ_COLS
    T = qT_ref.shape[-1]
    chunks = [slice(c * NC, (c + 1) * NC) for c in range(D // NC)]

    rnd = [0]
    queue = []

    def put(fn, *args):
        queue.append((rnd[0], functools.partial(fn, *args)))

    R = INPROJ_ROWS

    def row_group(j):
        rs = slice(j * R, (j + 1) * R)
        g_parts, u_parts, v_parts, stats = {}, {}, {}, {}
        x = x_ref[rs, :]
        y = x * lax.rsqrt(jnp.mean(x * x, axis=-1, keepdims=True) + RMS_EPS) * n1g_ref[...]
        h = (y * (1.0 + mod_ref[1:2, :]) + mod_ref[0:1, :]).astype(BF16)

        def epi_g(c, z):
            g = _gelu_tanh(z)
            g_parts[c] = g
            rsum = jnp.sum(g, axis=-1, keepdims=True)
            stats["sum"] = rsum if c == 0 else stats["sum"] + rsum
            if c == len(chunks) - 1:
                for cc in range(len(chunks)):
                    put(ln_center, cc)
                for cc in range(len(chunks)):
                    put(ln_scale, cc)

        def ln_center(c):
            gc = g_parts[c] - stats["sum"] * (1.0 / D)
            g_parts[c] = gc
            sq = jnp.sum(gc * gc, axis=-1, keepdims=True)
            stats["sq"] = sq if c == 0 else stats["sq"] + sq

        def ln_scale(c):
            rstd = lax.rsqrt(stats["sq"] * (1.0 / D) + LN_EPS)
            v = g_parts.pop(c) * rstd * lng_ref[:, chunks[c]] + lnb_ref[:, chunks[c]]
            v_parts[c] = v.astype(BF16)

        def epi_u(c, z):
            u_parts[c] = _gelu_tanh(z)

        def epi_k(c, z):
            k_ref[rs, chunks[c]] = z.astype(BF16)

        def epi_q(c, z):
            qz = (z * (LOG2E / math.sqrt(HEAD_DIM))).astype(BF16)
            for jj in range(R // T):
                qT_ref[j * (R // T) + jj, chunks[c], :] = qz[:, jj * T:(jj + 1) * T]

        def epi_v(c, z):
            vT_ref[chunks[c], rs] = z.astype(BF16)

        def epi_gate(ref, c, z):
            ref[rs, chunks[c]] = _sigmoid(z).astype(BF16)

        chunk_rows = [slice(ci * CHUNK, (ci + 1) * CHUNK) for ci in range(R // CHUNK)]

        def group_cols(gi):
            lo = gi * GROUP_WIDTH
            return lo // NC, slice(lo % NC, lo % NC + GROUP_WIDTH)

        def epi_sv(gi, sv):
            c, local = group_cols(gi)
            cols = slice(gi * GROUP_WIDTH, (gi + 1) * GROUP_WIDTH)
            for ci, rows in enumerate(chunk_rows):
                sv_c = sv[:, ci * GROUP_WIDTH:(ci + 1) * GROUP_WIDTH]
                out_rows = slice(j * R + rows.start, j * R + rows.stop)
                ob_ref[out_rows, cols] = (u_parts[c][rows, local] * sv_c).astype(BF16)

        def rows_dot(sec, cols):
            return jnp.dot(h, w_ref[:, sec * D + cols.start:sec * D + cols.stop],
                           preferred_element_type=F32)

        def feat_dot(wT_ref, rows):
            return lax.dot_general(wT_ref[rows, :], h, _NT, preferred_element_type=F32)

        heavy = ([(epi_g, (c,), rows_dot, (SEC_G, s)) for c, s in enumerate(chunks)]
                 + [(epi_u, (c,), rows_dot, (SEC_U, s)) for c, s in enumerate(chunks)]
                 + [(epi_gate, (sga_ref, c), rows_dot, (SEC_GA, s)) for c, s in enumerate(chunks)]
                 + [(epi_gate, (sgb_ref, c), rows_dot, (SEC_GB, s)) for c, s in enumerate(chunks)])
        light = ([(epi_k, (c,), rows_dot, (SEC_K, s)) for c, s in enumerate(chunks)]
                 + [(epi_q, (c,), feat_dot, (wqT_ref, s)) for c, s in enumerate(chunks)]
                 + [(epi_v, (c,), feat_dot, (wvT_ref, s)) for c, s in enumerate(chunks)])
        order = []
        while heavy or light:
            order += heavy[:1] + light[:1]
            heavy, light = heavy[1:], light[1:]
        tri = (lax.broadcasted_iota(jnp.int32, (CHUNK, CHUNK), 0)
               >= lax.broadcasted_iota(jnp.int32, (CHUNK, CHUNK), 1))

        def spatial(gi):
            ws = jnp.where(tri, ws_ref[gi], 0.0).astype(BF16)
            c, local = group_cols(gi)
            vcat = jnp.concatenate([v_parts[c][rows, local] for rows in chunk_rows], axis=1)
            sv = jnp.dot(ws, vcat, preferred_element_type=F32) + bs_ref[:, gi:gi + 1]
            put(epi_sv, gi, sv)

        spatial_todo = list(range(ws_ref.shape[0]))
        for epi, epi_args, dot_fn, dot_args in order:
            put(epi, *epi_args, dot_fn(*dot_args))
            yield True
            if spatial_todo and len(v_parts) == len(chunks) and len(u_parts) == len(chunks):
                spatial(spatial_todo.pop(0))
                yield True
        for gi in spatial_todo:
            spatial(gi)
            yield True

    live = [row_group(j) for j in range(tm // R)]
    while live or queue:
        live = [g for g in live if next(g, None) is not None]
        rnd[0] += 1
        ready = sum(1 for r, _ in queue if r < rnd[0] - 1) if live else len(queue)
        for _ in range(min(ready, 2 * len(live) if live else ready)):
            queue.pop(0)[1]()


def _inproj(x, mod3, norm1_g, w_in, ln_v_g, ln_v_b, w_spatial, b_spatial):
    B, S, D = x.shape
    T = SEQ_TILE
    nt = S // T
    tm = INPROJ_TILE
    w_bf = w_in.astype(BF16)
    wqT = w_bf[:, SEC_Q * D:(SEC_Q + 1) * D].T
    wvT = w_bf[:, SEC_V * D:(SEC_V + 1) * D].T
    G = w_spatial.shape[0]
    row = lambda a: a.reshape(1, D)
    tok = pl.BlockSpec((None, tm, D), lambda b, t: (b, t, 0))
    q_featT = pl.BlockSpec((None, tm // T, D, T), lambda b, t: (b, t, 0, 0))
    v_featT = pl.BlockSpec((None, D, tm), lambda b, t: (b, 0, t))
    out_tok = jax.ShapeDtypeStruct((B, S, D), BF16)
    return pl.pallas_call(
        _inproj_kernel,
        grid=(B, S // tm),
        in_specs=[tok,
                  pl.BlockSpec((None, 6, D), lambda b, t: (b, 0, 0)),
                  _resident((1, D)),
                  _resident((D, D)), _resident((D, D)), _resident(w_bf.shape),
                  _resident((1, D)), _resident((1, D)),
                  _resident((G, CHUNK, CHUNK)), _resident((CHUNK, G))],
        out_specs=[q_featT, tok, v_featT, tok, tok, tok],
        out_shape=[jax.ShapeDtypeStruct((B, nt, D, T), BF16), out_tok,
                   jax.ShapeDtypeStruct((B, D, S), BF16), out_tok, out_tok, out_tok],
        compiler_params=pltpu.CompilerParams(
            dimension_semantics=("arbitrary", "arbitrary"),
            vmem_limit_bytes=_vmem_limit(resident=(2 * D * D + w_bf.size) * 2,
                                         streamed=tm * D * 4 + 6 * tm * D * 2,
                                         temporaries=8 * tm * D * 4)),
        name="inproj",
    )(x, mod3, row(norm1_g), wqT, wvT, w_bf, row(ln_v_g), row(ln_v_b),
      w_spatial, b_spatial.T)


def _attn_kernel(qT_ref, k_ref, vT_ref, tbl_ref, lam_ref, sg_ref, o_ref, va_ref):
    nt, _, T = qT_ref.shape
    n_heads = va_ref.shape[0]
    feat = lax.broadcasted_iota(jnp.int32, (V_DIM, T), 0)
    head_cols = [slice(hl * V_DIM, (hl + 1) * V_DIM) for hl in range(n_heads)]
    for hl in range(n_heads):
        va_ref[hl, 0:V_DIM, :] = vT_ref[head_cols[hl], :]
        va_ref[hl, V_DIM:, :] = jnp.ones((ONES_ROWS, va_ref.shape[2]), BF16)
    lam = (jnp.exp(jnp.sum(lam_ref[0:1, :] * lam_ref[1:2, :], axis=-1, keepdims=True))
           - jnp.exp(jnp.sum(lam_ref[2:3, :] * lam_ref[3:4, :], axis=-1, keepdims=True))
           + LAM_INIT)

    units = [(hl, i, mi) for i in reversed(range(nt)) for hl in range(n_heads) for mi in range(2)]
    tick = [0]
    s_done = {}
    p_done = {}
    outs = {}

    def key_chunks(i):
        near_lo, hi = max(i - 1, 0) * T, (i + 1) * T
        cuts = sorted(set(range(near_lo, -1, -KEY_CHUNK)) | {0})
        far = [(lo, nxt, None) for lo, nxt in zip(cuts[:-1], cuts[1:])]
        return far + [(near_lo, hi, slice(2 * T - (hi - near_lo), 2 * T))]

    def scores():
        for u in units:
            hl, i, mi = u
            in_map = (feat < HEAD_DIM) if mi == 0 else (feat >= HEAD_DIM)
            qz = jnp.where(in_map, qT_ref[i, head_cols[hl], :].astype(F32), 0.0).astype(BF16)
            for c, (lo, hi, tbl_rows) in enumerate(key_chunks(i)):
                s = jnp.dot(k_ref[lo:hi, head_cols[hl]], qz, preferred_element_type=F32)
                if tbl_rows is not None:
                    s = s + tbl_ref[hl, tbl_rows, :]
                s_done[u, c] = (s, tick[0])
                yield True

    def probs():
        for u in units:
            for c in range(len(key_chunks(u[1]))):
                while (u, c) not in s_done or tick[0] < s_done[u, c][1] + EXP_LAG:
                    yield False
                s, _ = s_done.pop((u, c))
                mc = jnp.max(s, axis=0, keepdims=True)
                p_done[u, c] = (jnp.exp2((s - mc).astype(BF16)), mc, tick[0])
                yield True

    def values():
        for u in units:
            hl, i, mi = u
            parts = []
            for c, (lo, hi, _) in enumerate(key_chunks(i)):
                while (u, c) not in p_done or tick[0] < p_done[u, c][2] + VALUE_LAG:
                    yield False
                p, mc, _ = p_done.pop((u, c))
                parts.append((jnp.dot(va_ref[hl, :, lo:hi], p, preferred_element_type=F32), mc))
                yield True
            m = functools.reduce(jnp.maximum, [mc for _, mc in parts])
            acc = functools.reduce(lambda a, b: a + b, [d * jnp.exp2(mc - m) for d, mc in parts])
            outs[u] = acc[0:V_DIM, :] * (1.0 / acc[V_DIM:V_DIM + 1, :])
            if mi == 1:
                oT = outs.pop((hl, i, 0)) - lam * outs.pop((hl, i, 1))
                yT = oT * lax.rsqrt(jnp.mean(oT * oT, axis=0, keepdims=True) + RMS_EPS)
                y = yT.T * sg_ref[...] * (1.0 - LAM_INIT)
                o_ref[i * T:(i + 1) * T, head_cols[hl]] = y.astype(BF16)

    streams = [scores(), probs(), values()]
    while streams:
        streams = [g for g in streams if next(g, None) is not None]
        tick[0] += 1


def _attention(qT, k, vT, tbl, lam_vecs, subln_g):
    B, nt, D, T = qT.shape
    S = nt * T
    H = D // V_DIM
    hp = ATTN_HEADS_PER_STEP
    W = hp * V_DIM
    return pl.pallas_call(
        _attn_kernel,
        grid=(B, H // hp),
        in_specs=[pl.BlockSpec((None, nt, W, T), lambda b, h: (b, 0, h, 0)),
                  pl.BlockSpec((None, S, W), lambda b, h: (b, 0, h)),
                  pl.BlockSpec((None, W, S), lambda b, h: (b, h, 0)),
                  pl.BlockSpec((hp, 2 * T, T), lambda b, h: (h, 0, 0)),
                  pl.BlockSpec((4, HEAD_DIM), lambda b, h: (0, 0)),
                  pl.BlockSpec((1, V_DIM), lambda b, h: (0, 0))],
        out_specs=pl.BlockSpec((None, S, W), lambda b, h: (b, 0, h)),
        out_shape=jax.ShapeDtypeStruct((B, S, D), BF16),
        scratch_shapes=[pltpu.VMEM((hp, V_DIM + ONES_ROWS, S), BF16)],
        compiler_params=pltpu.CompilerParams(
            dimension_semantics=("arbitrary", "arbitrary"),
            vmem_limit_bytes=_vmem_limit(resident=hp * (V_DIM + ONES_ROWS) * S * 2,
                                         streamed=4 * S * W * 2 + hp * 2 * T * T * 4,
                                         temporaries=ATTN_TEMP_BYTES)),
        name="attn",
    )(qT, k, vT, tbl, lam_vecs, subln_g.reshape(1, V_DIM))


def _tail_kernel(x_ref, oa_ref, ob_ref, sga_ref, sgb_ref, mod_ref, n2g_ref, fg_ref,
                 wpa_ref, wpb_ref, wo_ref, wfi_ref, wfo_ref, o_ref):
    hidden = wfo_ref.shape[0]
    tm = x_ref.shape[0]
    bounds = list(range(0, hidden, TAIL_HIDDEN_CHUNK)) + [hidden]

    def row_group(rows):
        pa = jnp.dot(oa_ref[rows, :], wpa_ref[...], preferred_element_type=F32)
        pb = jnp.dot(ob_ref[rows, :], wpb_ref[...], preferred_element_type=F32)
        yield True
        merged = (sga_ref[rows, :].astype(F32) * pa + sgb_ref[rows, :].astype(F32) * pb)
        x1 = x_ref[rows, :] + mod_ref[2:3, :] * jnp.dot(merged.astype(BF16), wo_ref[...],
                                                        preferred_element_type=F32)
        yield True
        y = x1 * lax.rsqrt(jnp.mean(x1 * x1, axis=-1, keepdims=True) + RMS_EPS) * n2g_ref[...]
        h2 = (y * (1.0 + mod_ref[4:5, :]) + mod_ref[3:4, :]).astype(BF16)
        ffn = None
        for lo, hi in zip(bounds[:-1], bounds[1:]):
            gate = jnp.dot(h2, wfi_ref[:, lo:hi], preferred_element_type=F32)
            up = jnp.dot(h2, wfi_ref[:, hidden + lo:hidden + hi], preferred_element_type=F32)
            yield True
            act = (gate * _sigmoid(gate) * up).astype(BF16)
            part = jnp.dot(act, wfo_ref[lo:hi, :], preferred_element_type=F32)
            ffn = part if ffn is None else ffn + part
        yield True
        x2 = x1 + mod_ref[5:6, :] * ffn
        o_ref[rows, :] = (x2 * lax.rsqrt(jnp.mean(x2 * x2, axis=-1, keepdims=True) + RMS_EPS)
                          * fg_ref[...])

    live = [row_group(slice(r, r + TAIL_ROWS)) for r in range(0, tm, TAIL_ROWS)]
    while live:
        live = [g for g in live if next(g, None) is not None]


def _tail(x, oa, ob, sga, sgb, mod3, norm2_g, final_g, w_proj_a, w_proj_b, w_out, w_ffn_in,
          w_ffn_out):
    B, S, D = x.shape
    T = TAIL_TILE
    hidden = w_ffn_out.shape[0]
    row = lambda a: a.reshape(1, D)
    tok = pl.BlockSpec((None, T, D), lambda b, t: (b, t, 0))
    return pl.pallas_call(
        _tail_kernel,
        grid=(B, S // T),
        in_specs=[tok, tok, tok, tok, tok,
                  pl.BlockSpec((None, 6, D), lambda b, t: (b, 0, 0)),
                  _resident((1, D)), _resident((1, D)),
                  _resident((D, D)), _resident((D, D)), _resident((D, D)),
                  _resident((D, 2 * hidden)), _resident((hidden, D))],
        out_specs=tok,
        out_shape=jax.ShapeDtypeStruct((B, S, D), F32),
        compiler_params=pltpu.CompilerParams(
            dimension_semantics=("arbitrary", "arbitrary"),
            vmem_limit_bytes=_vmem_limit(
                resident=(3 * D * D + 3 * D * hidden) * 2,
                streamed=2 * T * D * 4 + 4 * T * D * 2,
                temporaries=(T // TAIL_ROWS) * TAIL_ROWS * (6 * D + 2 * TAIL_HIDDEN_CHUNK) * 4)),
        name="tail",
    )(x, oa, ob, sga, sgb, mod3, row(norm2_g), row(final_g),
      w_proj_a.astype(BF16), w_proj_b.astype(BF16), w_out.astype(BF16),
      w_ffn_in.astype(BF16), w_ffn_out.astype(BF16))


def kernel(x, c, w_ada, b_ada, norm1_g, norm2_g, w_in, lambda_q1, lambda_k1, lambda_q2, lambda_k2,
           subln_g, ln_v_g, ln_v_b, w_spatial, b_spatial, w_proj_a, w_proj_b, w_out, w_ffn_in,
           w_ffn_out, rel_bias, final_g):
    B, S, D = x.shape
    assert w_ada.shape[0] == 1, "single-layer block"
    assert S % INPROJ_TILE == 0 and S % TAIL_TILE == 0
    assert INPROJ_TILE % INPROJ_ROWS == 0 and INPROJ_ROWS % SEQ_TILE == 0
    assert INPROJ_ROWS % CHUNK == 0 and TAIL_TILE % TAIL_ROWS == 0
    assert SEQ_TILE >= REL_MAX_DIST, "bias must be constant beyond the first off-diagonal tile"

    mod3 = _ada(c, w_ada[0], b_ada[0]).reshape(B, 6, D)
    tbl = _bias_tiles(rel_bias, SEQ_TILE)
    qT, k, vT, ob, sga, sgb = _inproj(x, mod3, norm1_g[0], w_in[0], ln_v_g[0], ln_v_b[0],
                                      w_spatial[0], b_spatial[0])
    lam_vecs = jnp.concatenate([lambda_q1, lambda_k1, lambda_q2, lambda_k2], axis=0)
    oa = _attention(qT, k, vT, tbl, lam_vecs, subln_g[0])
    return _tail(x, oa, ob, sga, sgb, mod3, norm2_g[0], final_g, w_proj_a[0], w_proj_b[0],
                 w_out[0], w_ffn_in[0], w_ffn_out[0])
```

```python
import functools
import math

import numpy as np
import jax
import jax.numpy as jnp
from jax import lax
from jax.experimental import pallas as pl
from jax.experimental.pallas import tpu as pltpu

F32 = jnp.float32
BF16 = jnp.bfloat16

RMS_EPS = 1e-6
LN_EPS = 1e-5
HEAD_DIM = 64
V_DIM = 2 * HEAD_DIM
CHUNK = 128
GROUP_WIDTH = 128
REL_BUCKETS = 32
REL_MAX_EXACT = REL_BUCKETS // 2
REL_MAX_DIST = 128
LAM_INIT = 0.8 - 0.6 * math.exp(-0.3 * 0)
MASK_VALUE = -1e30
LOG2E = math.log2(math.e)
ONES_ROWS = 16
ATTN_HEADS_PER_STEP = 2
KEY_CHUNK = 256
EXP_LAG = 8
VALUE_LAG = 2

SEQ_TILE = 256
INPROJ_TILE = 512
INPROJ_ROWS = 512
INPROJ_COLS = 256
TAIL_TILE = 512
TAIL_ROWS = 256
TAIL_HIDDEN_CHUNK = 1536
V7X_VMEM_BYTES = 64 * 1024 * 1024
VMEM_HEADROOM = 6 * 1024 * 1024
ATTN_TEMP_BYTES = 46 * 1024 * 1024

_NT = (((1,), (1,)), ((), ()))
SEC_Q, SEC_K, SEC_V, SEC_U, SEC_G, SEC_GA, SEC_GB = range(7)


def _sigmoid(x):
    return 1.0 / (1.0 + jnp.exp2(x * (-LOG2E)))


def _gelu_tanh(x):
    a = -2.0 * math.sqrt(2.0 / math.pi) * LOG2E
    return x / (1.0 + jnp.exp2(x * (a + (a * 0.044715) * (x * x))))


def _vmem_limit(resident, streamed, temporaries):
    return int(min(resident + 2 * streamed + temporaries, V7X_VMEM_BYTES - VMEM_HEADROOM))


def _resident(shape):
    n = len(shape)
    return pl.BlockSpec(shape, lambda *_: (0,) * n, pipeline_mode=pl.Buffered(1))


def _ada_kernel(c_ref, w_ref, b_ref, o_ref):
    c = c_ref[...]
    ca = (c * _sigmoid(c)).astype(BF16)
    o_ref[...] = jnp.dot(ca, w_ref[...].astype(BF16), preferred_element_type=F32) + b_ref[...]


def _ada(c, w_ada, b_ada):
    B, D = c.shape
    N = w_ada.shape[1]
    tn = 1024
    return pl.pallas_call(
        _ada_kernel,
        grid=(N // tn,),
        in_specs=[pl.BlockSpec((B, D), lambda n: (0, 0)),
                  pl.BlockSpec((D, tn), lambda n: (0, n)),
                  pl.BlockSpec((1, tn), lambda n: (0, n))],
        out_specs=pl.BlockSpec((B, tn), lambda n: (0, n)),
        out_shape=jax.ShapeDtypeStruct((B, N), F32),
        name="ada",
    )(c, w_ada, b_ada.reshape(1, N))


def _bucket_tiles(T):
    kk = np.arange(T, dtype=np.int32)[:, None]
    qq = np.arange(T, dtype=np.int32)[None, :]
    out = []
    for o in (1, 0):
        dist = o * T + qq - kk
        n = np.maximum(dist, 0)
        nf = np.maximum(n, 1).astype(np.float32)
        large = REL_MAX_EXACT + (np.log(nf / np.float32(REL_MAX_EXACT))
                                 / np.float32(math.log(REL_MAX_DIST / REL_MAX_EXACT))
                                 * np.float32(REL_BUCKETS - REL_MAX_EXACT)).astype(np.int32)
        large = np.minimum(large, REL_BUCKETS - 1)
        b = np.where(n < REL_MAX_EXACT, n, large)
        out.append(np.where(dist < 0, -1, b).astype(np.int32))
    return np.concatenate(out, axis=0)


def _bias_kernel(rb_ref, bucket_ref, o_ref):
    h = pl.program_id(0)
    bk = bucket_ref[...]
    far = rb_ref[REL_BUCKETS - 1, h]
    acc = jnp.where(bk < 0, MASK_VALUE, 0.0).astype(F32)
    for b in range(REL_BUCKETS - 1):
        acc = jnp.where(bk == b, (rb_ref[b, h] - far) * LOG2E, acc)
    o_ref[...] = acc


def _bias_tiles(rel_bias, T):
    H = rel_bias.shape[1]
    buckets = jnp.asarray(_bucket_tiles(T))
    return pl.pallas_call(
        _bias_kernel,
        grid=(H,),
        in_specs=[pl.BlockSpec(memory_space=pltpu.SMEM),
                  pl.BlockSpec((2 * T, T), lambda h: (0, 0))],
        out_specs=pl.BlockSpec((None, 2 * T, T), lambda h: (h, 0, 0)),
        out_shape=jax.ShapeDtypeStruct((H, 2 * T, T), F32),
        name="bias_tiles",
    )(rel_bias, buckets)


def _inproj_kernel(x_ref, mod_ref, n1g_ref, wqT_ref, wvT_ref, w_ref, lng_ref, lnb_ref, ws_ref, bs_ref,
                   qT_ref, k_ref, vT_ref, ob_ref, sga_ref, sgb_ref):
    tm, D = x_ref.shape
    NC = INPROJ_COLS
    T = qT_ref.shape[-1]
    chunks = [slice(c * NC, (c + 1) * NC) for c in range(D // NC)]

    rnd = [0]
    queue = []

    def put(fn, *args):
        queue.append((rnd[0], functools.partial(fn, *args)))

    R = INPROJ_ROWS

    def row_group(j):
        rs = slice(j * R, (j + 1) * R)
        g_parts, u_parts, v_parts, stats = {}, {}, {}, {}
        x = x_ref[rs, :]
        y = x * lax.rsqrt(jnp.mean(x * x, axis=-1, keepdims=True) + RMS_EPS) * n1g_ref[...]
        h = (y * (1.0 + mod_ref[1:2, :]) + mod_ref[0:1, :]).astype(BF16)

        def epi_g(c, z):
            g = _gelu_tanh(z)
            g_parts[c] = g
            rsum = jnp.sum(g, axis=-1, keepdims=True)
            stats["sum"] = rsum if c == 0 else stats["sum"] + rsum
            if c == len(chunks) - 1:
                for cc in range(len(chunks)):
                    put(ln_center, cc)
                for cc in range(len(chunks)):
                    put(ln_scale, cc)

        def ln_center(c):
            gc = g_parts[c] - stats["sum"] * (1.0 / D)
            g_parts[c] = gc
            sq = jnp.sum(gc * gc, axis=-1, keepdims=True)
            stats["sq"] = sq if c == 0 else stats["sq"] + sq

        def ln_scale(c):
            rstd = lax.rsqrt(stats["sq"] * (1.0 / D) + LN_EPS)
            v = g_parts.pop(c) * rstd * lng_ref[:, chunks[c]] + lnb_ref[:, chunks[c]]
            v_parts[c] = v.astype(BF16)

        def epi_u(c, z):
            u_parts[c] = _gelu_tanh(z)

        def epi_k(c, z):
            k_ref[rs, chunks[c]] = z.astype(BF16)

        def epi_q(c, z):
            qz = (z * (LOG2E / math.sqrt(HEAD_DIM))).astype(BF16)
            for jj in range(R // T):
                qT_ref[j * (R // T) + jj, chunks[c], :] = qz[:, jj * T:(jj + 1) * T]

        def epi_v(c, z):
            vT_ref[chunks[c], rs] = z.astype(BF16)

        def epi_gate(ref, c, z):
            ref[rs, chunks[c]] = _sigmoid(z).astype(BF16)

        chunk_rows = [slice(ci * CHUNK, (ci + 1) * CHUNK) for ci in range(R // CHUNK)]

        def group_cols(gi):
            lo = gi * GROUP_WIDTH
            return lo // NC, slice(lo % NC, lo % NC + GROUP_WIDTH)

        def epi_sv(gi, sv):
            c, local = group_cols(gi)
            cols = slice(gi * GROUP_WIDTH, (gi + 1) * GROUP_WIDTH)
            for ci, rows in enumerate(chunk_rows):
                sv_c = sv[:, ci * GROUP_WIDTH:(ci + 1) * GROUP_WIDTH]
                out_rows = slice(j * R + rows.start, j * R + rows.stop)
                ob_ref[out_rows, cols] = (u_parts[c][rows, local] * sv_c).astype(BF16)

        def rows_dot(sec, cols):
            return jnp.dot(h, w_ref[:, sec * D + cols.start:sec * D + cols.stop],
                           preferred_element_type=F32)

        def feat_dot(wT_ref, rows):
            return lax.dot_general(wT_ref[rows, :], h, _NT, preferred_element_type=F32)

        heavy = ([(epi_g, (c,), rows_dot, (SEC_G, s)) for c, s in enumerate(chunks)]
                 + [(epi_u, (c,), rows_dot, (SEC_U, s)) for c, s in enumerate(chunks)]
                 + [(epi_gate, (sga_ref, c), rows_dot, (SEC_GA, s)) for c, s in enumerate(chunks)]
                 + [(epi_gate, (sgb_ref, c), rows_dot, (SEC_GB, s)) for c, s in enumerate(chunks)])
        light = ([(epi_k, (c,), rows_dot, (SEC_K, s)) for c, s in enumerate(chunks)]
                 + [(epi_q, (c,), feat_dot, (wqT_ref, s)) for c, s in enumerate(chunks)]
                 + [(epi_v, (c,), feat_dot, (wvT_ref, s)) for c, s in enumerate(chunks)])
        order = []
        while heavy or light:
            order += heavy[:1] + light[:1]
            heavy, light = heavy[1:], light[1:]
        tri = (lax.broadcasted_iota(jnp.int32, (CHUNK, CHUNK), 0)
               >= lax.broadcasted_iota(jnp.int32, (CHUNK, CHUNK), 1))

        def spatial(gi):
            ws = jnp.where(tri, ws_ref[gi], 0.0).astype(BF16)
            c, local = group_cols(gi)
            vcat = jnp.concatenate([v_parts[c][rows, local] for rows in chunk_rows], axis=1)
            sv = jnp.dot(ws, vcat, preferred_element_type=F32) + bs_ref[:, gi:gi + 1]
            put(epi_sv, gi, sv)

        spatial_todo = list(range(ws_ref.shape[0]))
        for epi, epi_args, dot_fn, dot_args in order:
            put(epi, *epi_args, dot_fn(*dot_args))
            yield True
            if spatial_todo and len(v_parts) == len(chunks) and len(u_parts) == len(chunks):
                spatial(spatial_todo.pop(0))
                yield True
        for gi in spatial_todo:
            spatial(gi)
            yield True

    live = [row_group(j) for j in range(tm // R)]
    while live or queue:
        live = [g for g in live if next(g, None) is not None]
        rnd[0] += 1
        ready = sum(1 for r, _ in queue if r < rnd[0] - 1) if live else len(queue)
        for _ in range(min(ready, 2 * len(live) if live else ready)):
            queue.pop(0)[1]()


def _inproj(x, mod3, norm1_g, w_in, ln_v_g, ln_v_b, w_spatial, b_spatial):
    B, S, D = x.shape
    T = SEQ_TILE
    nt = S // T
    tm = INPROJ_TILE
    w_bf = w_in.astype(BF16)
    wqT = w_bf[:, SEC_Q * D:(SEC_Q + 1) * D].T
    wvT = w_bf[:, SEC_V * D:(SEC_V + 1) * D].T
    G = w_spatial.shape[0]
    row = lambda a: a.reshape(1, D)
    tok = pl.BlockSpec((None, tm, D), lambda b, t: (b, t, 0))
    q_featT = pl.BlockSpec((None, tm // T, D, T), lambda b, t: (b, t, 0, 0))
    v_featT = pl.BlockSpec((None, D, tm), lambda b, t: (b, 0, t))
    out_tok = jax.ShapeDtypeStruct((B, S, D), BF16)
    return pl.pallas_call(
        _inproj_kernel,
        grid=(B, S // tm),
        in_specs=[tok,
                  pl.BlockSpec((None, 6, D), lambda b, t: (b, 0, 0)),
                  _resident((1, D)),
                  _resident((D, D)), _resident((D, D)), _resident(w_bf.shape),
                  _resident((1, D)), _resident((1, D)),
                  _resident((G, CHUNK, CHUNK)), _resident((CHUNK, G))],
        out_specs=[q_featT, tok, v_featT, tok, tok, tok],
        out_shape=[jax.ShapeDtypeStruct((B, nt, D, T), BF16), out_tok,
                   jax.ShapeDtypeStruct((B, D, S), BF16), out_tok, out_tok, out_tok],
        compiler_params=pltpu.CompilerParams(
            dimension_semantics=("arbitrary", "arbitrary"),
            vmem_limit_bytes=_vmem_limit(resident=(2 * D * D + w_bf.size) * 2,
                                         streamed=tm * D * 4 + 6 * tm * D * 2,
                                         temporaries=8 * tm * D * 4)),
        name="inproj",
    )(x, mod3, row(norm1_g), wqT, wvT, w_bf, row(ln_v_g), row(ln_v_b),
      w_spatial, b_spatial.T)


def _attn_kernel(qT_ref, k_ref, vT_ref, tbl_ref, lam_ref, sg_ref, o_ref, va_ref):
    nt, _, T = qT_ref.shape
    n_heads = va_ref.shape[0]
    feat = lax.broadcasted_iota(jnp.int32, (V_DIM, T), 0)
    head_cols = [slice(hl * V_DIM, (hl + 1) * V_DIM) for hl in range(n_heads)]
    for hl in range(n_heads):
        va_ref[hl, 0:V_DIM, :] = vT_ref[head_cols[hl], :]
        va_ref[hl, V_DIM:, :] = jnp.ones((ONES_ROWS, va_ref.shape[2]), BF16)
    lam = (jnp.exp(jnp.sum(lam_ref[0:1, :] * lam_ref[1:2, :], axis=-1, keepdims=True))
           - jnp.exp(jnp.sum(lam_ref[2:3, :] * lam_ref[3:4, :], axis=-1, keepdims=True))
           + LAM_INIT)

    units = [(hl, i, mi) for i in reversed(range(nt)) for hl in range(n_heads) for mi in range(2)]
    tick = [0]
    s_done = {}
    p_done = {}
    outs = {}

    def key_chunks(i):
        near_lo, hi = max(i - 1, 0) * T, (i + 1) * T
        cuts = sorted(set(range(near_lo, -1, -KEY_CHUNK)) | {0})
        far = [(lo, nxt, None) for lo, nxt in zip(cuts[:-1], cuts[1:])]
        return far + [(near_lo, hi, slice(2 * T - (hi - near_lo), 2 * T))]

    def scores():
        for u in units:
            hl, i, mi = u
            in_map = (feat < HEAD_DIM) if mi == 0 else (feat >= HEAD_DIM)
            qz = jnp.where(in_map, qT_ref[i, head_cols[hl], :].astype(F32), 0.0).astype(BF16)
            for c, (lo, hi, tbl_rows) in enumerate(key_chunks(i)):
                s = jnp.dot(k_ref[lo:hi, head_cols[hl]], qz, preferred_element_type=F32)
                if tbl_rows is not None:
                    s = s + tbl_ref[hl, tbl_rows, :]
                s_done[u, c] = (s, tick[0])
                yield True

    def probs():
        for u in units:
            for c in range(len(key_chunks(u[1]))):
                while (u, c) not in s_done or tick[0] < s_done[u, c][1] + EXP_LAG:
                    yield False
                s, _ = s_done.pop((u, c))
                mc = jnp.max(s, axis=0, keepdims=True)
                p_done[u, c] = (jnp.exp2((s - mc).astype(BF16)), mc, tick[0])
                yield True

    def values():
        for u in units:
            hl, i, mi = u
            parts = []
            for c, (lo, hi, _) in enumerate(key_chunks(i)):
                while (u, c) not in p_done or tick[0] < p_done[u, c][2] + VALUE_LAG:
                    yield False
                p, mc, _ = p_done.pop((u, c))
                parts.append((jnp.dot(va_ref[hl, :, lo:hi], p, preferred_element_type=F32), mc))
                yield True
            m = functools.reduce(jnp.maximum, [mc for _, mc in parts])
            acc = functools.reduce(lambda a, b: a + b, [d * jnp.exp2(mc - m) for d, mc in parts])
            outs[u] = acc[0:V_DIM, :] * (1.0 / acc[V_DIM:V_DIM + 1, :])
            if mi == 1:
                oT = outs.pop((hl, i, 0)) - lam * outs.pop((hl, i, 1))
                yT = oT * lax.rsqrt(jnp.mean(oT * oT, axis=0, keepdims=True) + RMS_EPS)
                y = yT.T * sg_ref[...] * (1.0 - LAM_INIT)
                o_ref[i * T:(i + 1) * T, head_cols[hl]] = y.astype(BF16)

    streams = [scores(), probs(), values()]
    while streams:
        streams = [g for g in streams if next(g, None) is not None]
        tick[0] += 1


def _attention(qT, k, vT, tbl, lam_vecs, subln_g):
    B, nt, D, T = qT.shape
    S = nt * T
    H = D // V_DIM
    hp = ATTN_HEADS_PER_STEP
    W = hp * V_DIM
    return pl.pallas_call(
        _attn_kernel,
        grid=(B, H // hp),
        in_specs=[pl.BlockSpec((None, nt, W, T), lambda b, h: (b, 0, h, 0)),
                  pl.BlockSpec((None, S, W), lambda b, h: (b, 0, h)),
                  pl.BlockSpec((None, W, S), lambda b, h: (b, h, 0)),
                  pl.BlockSpec((hp, 2 * T, T), lambda b, h: (h, 0, 0)),
                  pl.BlockSpec((4, HEAD_DIM), lambda b, h: (0, 0)),
                  pl.BlockSpec((1, V_DIM), lambda b, h: (0, 0))],
        out_specs=pl.BlockSpec((None, S, W), lambda b, h: (b, 0, h)),
        out_shape=jax.ShapeDtypeStruct((B, S, D), BF16),
        scratch_shapes=[pltpu.VMEM((hp, V_DIM + ONES_ROWS, S), BF16)],
        compiler_params=pltpu.CompilerParams(
            dimension_semantics=("arbitrary", "arbitrary"),
            vmem_limit_bytes=_vmem_limit(resident=hp * (V_DIM + ONES_ROWS) * S * 2,
                                         streamed=4 * S * W * 2 + hp * 2 * T * T * 4,
                                         temporaries=ATTN_TEMP_BYTES)),
        name="attn",
    )(qT, k, vT, tbl, lam_vecs, subln_g.reshape(1, V_DIM))


def _tail_kernel(x_ref, oa_ref, ob_ref, sga_ref, sgb_ref, mod_ref, n2g_ref, fg_ref,
                 wpa_ref, wpb_ref, wo_ref, wfi_ref, wfo_ref, o_ref):
    hidden = wfo_ref.shape[0]
    tm = x_ref.shape[0]
    bounds = list(range(0, hidden, TAIL_HIDDEN_CHUNK)) + [hidden]

    def row_group(rows):
        pa = jnp.dot(oa_ref[rows, :], wpa_ref[...], preferred_element_type=F32)
        pb = jnp.dot(ob_ref[rows, :], wpb_ref[...], preferred_element_type=F32)
        yield True
        merged = (sga_ref[rows, :].astype(F32) * pa + sgb_ref[rows, :].astype(F32) * pb)
        x1 = x_ref[rows, :] + mod_ref[2:3, :] * jnp.dot(merged.astype(BF16), wo_ref[...],
                                                        preferred_element_type=F32)
        yield True
        y = x1 * lax.rsqrt(jnp.mean(x1 * x1, axis=-1, keepdims=True) + RMS_EPS) * n2g_ref[...]
        h2 = (y * (1.0 + mod_ref[4:5, :]) + mod_ref[3:4, :]).astype(BF16)
        ffn = None
        for lo, hi in zip(bounds[:-1], bounds[1:]):
            gate = jnp.dot(h2, wfi_ref[:, lo:hi], preferred_element_type=F32)
            up = jnp.dot(h2, wfi_ref[:, hidden + lo:hidden + hi], preferred_element_type=F32)
            yield True
            act = (gate * _sigmoid(gate) * up).astype(BF16)
            part = jnp.dot(act, wfo_ref[lo:hi, :], preferred_element_type=F32)
            ffn = part if ffn is None else ffn + part
        yield True
        x2 = x1 + mod_ref[5:6, :] * ffn
        o_ref[rows, :] = (x2 * lax.rsqrt(jnp.mean(x2 * x2, axis=-1, keepdims=True) + RMS_EPS)
                          * fg_ref[...])

    live = [row_group(slice(r, r + TAIL_ROWS)) for r in range(0, tm, TAIL_ROWS)]
    while live:
        live = [g for g in live if next(g, None) is not None]


def _tail(x, oa, ob, sga, sgb, mod3, norm2_g, final_g, w_proj_a, w_proj_b, w_out, w_ffn_in,
          w_ffn_out):
    B, S, D = x.shape
    T = TAIL_TILE
    hidden = w_ffn_out.shape[0]
    row = lambda a: a.reshape(1, D)
    tok = pl.BlockSpec((None, T, D), lambda b, t: (b, t, 0))
    return pl.pallas_call(
        _tail_kernel,
        grid=(B, S // T),
        in_specs=[tok, tok, tok, tok, tok,
                  pl.BlockSpec((None, 6, D), lambda b, t: (b, 0, 0)),
                  _resident((1, D)), _resident((1, D)),
                  _resident((D, D)), _resident((D, D)), _resident((D, D)),
                  _resident((D, 2 * hidden)), _resident((hidden, D))],
        out_specs=tok,
        out_shape=jax.ShapeDtypeStruct((B, S, D), F32),
        compiler_params=pltpu.CompilerParams(
            dimension_semantics=("arbitrary", "arbitrary"),
            vmem_limit_bytes=_vmem_limit(
                resident=(3 * D * D + 3 * D * hidden) * 2,
                streamed=2 * T * D * 4 + 4 * T * D * 2,
                temporaries=(T // TAIL_ROWS) * TAIL_ROWS * (6 * D + 2 * TAIL_HIDDEN_CHUNK) * 4)),
        name="tail",
    )(x, oa, ob, sga, sgb, mod3, row(norm2_g), row(final_g),
      w_proj_a.astype(BF16), w_proj_b.astype(BF16), w_out.astype(BF16),
      w_ffn_in.astype(BF16), w_ffn_out.astype(BF16))


def kernel(x, c, w_ada, b_ada, norm1_g, norm2_g, w_in, lambda_q1, lambda_k1, lambda_q2, lambda_k2,
           subln_g, ln_v_g, ln_v_b, w_spatial, b_spatial, w_proj_a, w_proj_b, w_out, w_ffn_in,
           w_ffn_out, rel_bias, final_g):
    B, S, D = x.shape
    assert w_ada.shape[0] == 1, "single-layer block"
    assert S % INPROJ_TILE == 0 and S % TAIL_TILE == 0
    assert INPROJ_TILE % INPROJ_ROWS == 0 and INPROJ_ROWS % SEQ_TILE == 0
    assert INPROJ_ROWS % CHUNK == 0 and TAIL_TILE % TAIL_ROWS == 0
    assert SEQ_TILE >= REL_MAX_DIST, "bias must be constant beyond the first off-diagonal tile"

    mod3 = _ada(c, w_ada[0], b_ada[0]).reshape(B, 6, D)
    tbl = _bias_tiles(rel_bias, SEQ_TILE)
    qT, k, vT, ob, sga, sgb = _inproj(x, mod3, norm1_g[0], w_in[0], ln_v_g[0], ln_v_b[0],
                                      w_spatial[0], b_spatial[0])
    lam_vecs = jnp.concatenate([lambda_q1, lambda_k1, lambda_q2, lambda_k2], axis=0)
    oa = _attention(qT, k, vT, tbl, lam_vecs, subln_g[0])
    return _tail(x, oa, ob, sga, sgb, mod3, norm2_g[0], final_g, w_proj_a[0], w_proj_b[0],
                 w_out[0], w_ffn_in[0], w_ffn_out[0])
```

```python
import functools
import math

import numpy as np
import jax
import jax.numpy as jnp
from jax import lax
from jax.experimental import pallas as pl
from jax.experimental.pallas import tpu as pltpu

F32 = jnp.float32
BF16 = jnp.bfloat16

RMS_EPS = 1e-6
LN_EPS = 1e-5
HEAD_DIM = 64
V_DIM = 2 * HEAD_DIM
CHUNK = 128
GROUP_WIDTH = 128
REL_BUCKETS = 32
REL_MAX_EXACT = REL_BUCKETS // 2
REL_MAX_DIST = 128
LAM_INIT = 0.8 - 0.6 * math.exp(-0.3 * 0)
MASK_VALUE = -1e30
LOG2E = math.log2(math.e)
ONES_ROWS = 16
ATTN_HEADS_PER_STEP = 2
KEY_CHUNK = 256
EXP_LAG = 8
VALUE_LAG = 2

SEQ_TILE = 256
INPROJ_TILE = 512
INPROJ_ROWS = 512
INPROJ_SLAB = 128
INPROJ_EARLY_CHUNKS = 2
INPROJ_COLS = 256
TAIL_TILE = 512
TAIL_ROWS = 256
TAIL_HIDDEN_CHUNK = 1536
V7X_VMEM_BYTES = 64 * 1024 * 1024
VMEM_HEADROOM = 6 * 1024 * 1024
ATTN_TEMP_BYTES = 46 * 1024 * 1024

_NT = (((1,), (1,)), ((), ()))
SEC_Q, SEC_K, SEC_V, SEC_U, SEC_G, SEC_GA, SEC_GB = range(7)


def _sigmoid(x):
    return 1.0 / (1.0 + jnp.exp2(x * (-LOG2E)))


def _gelu_tanh(x):
    a = -2.0 * math.sqrt(2.0 / math.pi) * LOG2E
    return x / (1.0 + jnp.exp2(x * (a + (a * 0.044715) * (x * x))))


def _vmem_limit(resident, streamed, temporaries):
    return int(min(resident + 2 * streamed + temporaries, V7X_VMEM_BYTES - VMEM_HEADROOM))


def _resident(shape):
    n = len(shape)
    return pl.BlockSpec(shape, lambda *_: (0,) * n, pipeline_mode=pl.Buffered(1))


def _ada_kernel(c_ref, w_ref, b_ref, o_ref):
    c = c_ref[...]
    ca = (c * _sigmoid(c)).astype(BF16)
    o_ref[...] = jnp.dot(ca, w_ref[...].astype(BF16), preferred_element_type=F32) + b_ref[...]


def _ada(c, w_ada, b_ada):
    B, D = c.shape
    N = w_ada.shape[1]
    tn = 1024
    return pl.pallas_call(
        _ada_kernel,
        grid=(N // tn,),
        in_specs=[pl.BlockSpec((B, D), lambda n: (0, 0)),
                  pl.BlockSpec((D, tn), lambda n: (0, n)),
                  pl.BlockSpec((1, tn), lambda n: (0, n))],
        out_specs=pl.BlockSpec((B, tn), lambda n: (0, n)),
        out_shape=jax.ShapeDtypeStruct((B, N), F32),
        name="ada",
    )(c, w_ada, b_ada.reshape(1, N))


def _bucket_tiles(T):
    kk = np.arange(T, dtype=np.int32)[:, None]
    qq = np.arange(T, dtype=np.int32)[None, :]
    out = []
    for o in (1, 0):
        dist = o * T + qq - kk
        n = np.maximum(dist, 0)
        nf = np.maximum(n, 1).astype(np.float32)
        large = REL_MAX_EXACT + (np.log(nf / np.float32(REL_MAX_EXACT))
                                 / np.float32(math.log(REL_MAX_DIST / REL_MAX_EXACT))
                                 * np.float32(REL_BUCKETS - REL_MAX_EXACT)).astype(np.int32)
        large = np.minimum(large, REL_BUCKETS - 1)
        b = np.where(n < REL_MAX_EXACT, n, large)
        out.append(np.where(dist < 0, -1, b).astype(np.int32))
    return np.concatenate(out, axis=0)


def _bias_kernel(rb_ref, bucket_ref, o_ref):
    h = pl.program_id(0)
    bk = bucket_ref[...]
    far = rb_ref[REL_BUCKETS - 1, h]
    acc = jnp.where(bk < 0, MASK_VALUE, 0.0).astype(F32)
    for b in range(REL_BUCKETS - 1):
        acc = jnp.where(bk == b, (rb_ref[b, h] - far) * LOG2E, acc)
    o_ref[...] = acc


def _bias_tiles(rel_bias, T):
    H = rel_bias.shape[1]
    buckets = jnp.asarray(_bucket_tiles(T))
    return pl.pallas_call(
        _bias_kernel,
        grid=(H,),
        in_specs=[pl.BlockSpec(memory_space=pltpu.SMEM),
                  pl.BlockSpec((2 * T, T), lambda h: (0, 0))],
        out_specs=pl.BlockSpec((None, 2 * T, T), lambda h: (h, 0, 0)),
        out_shape=jax.ShapeDtypeStruct((H, 2 * T, T), F32),
        name="bias_tiles",
    )(rel_bias, buckets)


def _inproj_kernel(x_ref, mod_ref, n1g_ref, wqT_ref, wvT_ref, w_ref, lng_ref, lnb_ref, ws_ref, bs_ref,
                   qT_ref, k_ref, vT_ref, ob_ref, sga_ref, sgb_ref):
    tm, D = x_ref.shape
    NC = INPROJ_COLS
    T = qT_ref.shape[-1]
    chunks = [slice(c * NC, (c + 1) * NC) for c in range(D // NC)]

    rnd = [0]
    queue = []

    def put(fn, *args):
        queue.append((rnd[0], functools.partial(fn, *args)))

    R = INPROJ_ROWS

    def row_group(j):
        rs = slice(j * R, (j + 1) * R)
        g_parts, u_parts, v_parts, stats = {}, {}, {}, {}

        def normed(rows):
            x = x_ref[rs.start + rows.start:rs.start + rows.stop, :]
            y = x * lax.rsqrt(jnp.mean(x * x, axis=-1, keepdims=True) + RMS_EPS) * n1g_ref[...]
            return (y * (1.0 + mod_ref[1:2, :]) + mod_ref[0:1, :]).astype(BF16)

        def epi_g(c, z):
            g = _gelu_tanh(z)
            g_parts[c] = g
            rsum = jnp.sum(g, axis=-1, keepdims=True)
            stats["sum"] = rsum if c == 0 else stats["sum"] + rsum
            if c == len(chunks) - 1:
                for cc in range(len(chunks)):
                    put(ln_center, cc)
                for cc in range(len(chunks)):
                    put(ln_scale, cc)

        def ln_center(c):
            gc = g_parts[c] - stats["sum"] * (1.0 / D)
            g_parts[c] = gc
            sq = jnp.sum(gc * gc, axis=-1, keepdims=True)
            stats["sq"] = sq if c == 0 else stats["sq"] + sq

        def ln_scale(c):
            rstd = lax.rsqrt(stats["sq"] * (1.0 / D) + LN_EPS)
            v = g_parts.pop(c) * rstd * lng_ref[:, chunks[c]] + lnb_ref[:, chunks[c]]
            v_parts[c] = v.astype(BF16)

        def epi_u(c, z):
            u_parts[c] = _gelu_tanh(z)

        def epi_k(c, z):
            k_ref[rs, chunks[c]] = z.astype(BF16)

        def epi_q(c, z):
            qz = (z * (LOG2E / math.sqrt(HEAD_DIM))).astype(BF16)
            for jj in range(R // T):
                qT_ref[j * (R // T) + jj, chunks[c], :] = qz[:, jj * T:(jj + 1) * T]

        def epi_v(c, z):
            vT_ref[chunks[c], rs] = z.astype(BF16)

        def epi_gate(ref, c, z):
            ref[rs, chunks[c]] = _sigmoid(z).astype(BF16)

        chunk_rows = [slice(ci * CHUNK, (ci + 1) * CHUNK) for ci in range(R // CHUNK)]

        def group_cols(gi):
            lo = gi * GROUP_WIDTH
            return lo // NC, slice(lo % NC, lo % NC + GROUP_WIDTH)

        def epi_sv(gi, sv):
            c, local = group_cols(gi)
            cols = slice(gi * GROUP_WIDTH, (gi + 1) * GROUP_WIDTH)
            for ci, rows in enumerate(chunk_rows):
                sv_c = sv[:, ci * GROUP_WIDTH:(ci + 1) * GROUP_WIDTH]
                out_rows = slice(j * R + rows.start, j * R + rows.stop)
                ob_ref[out_rows, cols] = (u_parts[c][rows, local] * sv_c).astype(BF16)

        def rows_dot(sec, cols, lhs=None):
            return jnp.dot(h if lhs is None else lhs,
                           w_ref[:, sec * D + cols.start:sec * D + cols.stop],
                           preferred_element_type=F32)

        def feat_dot(wT_ref, rows):
            return lax.dot_general(wT_ref[rows, :], h, _NT, preferred_element_type=F32)

        heavy = ([(epi_g, (c,), rows_dot, (SEC_G, s)) for c, s in enumerate(chunks)]
                 + [(epi_u, (c,), rows_dot, (SEC_U, s)) for c, s in enumerate(chunks)]
                 + [(epi_gate, (sga_ref, c), rows_dot, (SEC_GA, s)) for c, s in enumerate(chunks)]
                 + [(epi_gate, (sgb_ref, c), rows_dot, (SEC_GB, s)) for c, s in enumerate(chunks)])
        light = ([(epi_k, (c,), rows_dot, (SEC_K, s)) for c, s in enumerate(chunks)]
                 + [(epi_q, (c,), feat_dot, (wqT_ref, s)) for c, s in enumerate(chunks)]
                 + [(epi_v, (c,), feat_dot, (wvT_ref, s)) for c, s in enumerate(chunks)])
        order = []
        while heavy or light:
            order += heavy[:1] + light[:1]
            heavy, light = heavy[1:], light[1:]
        tri = (lax.broadcasted_iota(jnp.int32, (CHUNK, CHUNK), 0)
               >= lax.broadcasted_iota(jnp.int32, (CHUNK, CHUNK), 1))

        def spatial(gi):
            ws = jnp.where(tri, ws_ref[gi], 0.0).astype(BF16)
            c, local = group_cols(gi)
            vcat = jnp.concatenate([v_parts[c][rows, local] for rows in chunk_rows], axis=1)
            sv = jnp.dot(ws, vcat, preferred_element_type=F32) + bs_ref[:, gi:gi + 1]
            put(epi_sv, gi, sv)

        early, order = order[:INPROJ_EARLY_CHUNKS], order[INPROJ_EARLY_CHUNKS:]
        assert all(dot_fn is rows_dot for _, _, dot_fn, _ in early)
        h_slabs, z_slabs = [], [[] for _ in early]
        for lo in range(0, R, INPROJ_SLAB):
            h_slabs.append(normed(slice(lo, lo + INPROJ_SLAB)))
            for zs, (_, _, _, dot_args) in zip(z_slabs, early):
                zs.append(rows_dot(*dot_args, lhs=h_slabs[-1]))
        h = jnp.concatenate(h_slabs, axis=0)
        for zs, (epi, epi_args, _, _) in zip(z_slabs, early):
            put(epi, *epi_args, jnp.concatenate(zs, axis=0))
        yield True

        spatial_todo = list(range(ws_ref.shape[0]))
        for epi, epi_args, dot_fn, dot_args in order:
            put(epi, *epi_args, dot_fn(*dot_args))
            yield True
            if spatial_todo and len(v_parts) == len(chunks) and len(u_parts) == len(chunks):
                spatial(spatial_todo.pop(0))
                yield True
        for gi in spatial_todo:
            spatial(gi)
            yield True

    live = [row_group(j) for j in range(tm // R)]
    while live or queue:
        live = [g for g in live if next(g, None) is not None]
        rnd[0] += 1
        ready = sum(1 for r, _ in queue if r < rnd[0] - 1) if live else len(queue)
        for _ in range(min(ready, 2 * len(live) if live else ready)):
            queue.pop(0)[1]()


def _inproj(x, mod3, norm1_g, w_in, ln_v_g, ln_v_b, w_spatial, b_spatial):
    B, S, D = x.shape
    T = SEQ_TILE
    nt = S // T
    tm = INPROJ_TILE
    w_bf = w_in.astype(BF16)
    wqT = w_bf[:, SEC_Q * D:(SEC_Q + 1) * D].T
    wvT = w_bf[:, SEC_V * D:(SEC_V + 1) * D].T
    G = w_spatial.shape[0]
    row = lambda a: a.reshape(1, D)
    tok = pl.BlockSpec((None, tm, D), lambda b, t: (b, t, 0))
    q_featT = pl.BlockSpec((None, tm // T, D, T), lambda b, t: (b, t, 0, 0))
    v_featT = pl.BlockSpec((None, D, tm), lambda b, t: (b, 0, t))
    out_tok = jax.ShapeDtypeStruct((B, S, D), BF16)
    return pl.pallas_call(
        _inproj_kernel,
        grid=(B, S // tm),
        in_specs=[tok,
                  pl.BlockSpec((None, 6, D), lambda b, t: (b, 0, 0)),
                  _resident((1, D)),
                  _resident((D, D)), _resident((D, D)), _resident(w_bf.shape),
                  _resident((1, D)), _resident((1, D)),
                  _resident((G, CHUNK, CHUNK)), _resident((CHUNK, G))],
        out_specs=[q_featT, tok, v_featT, tok, tok, tok],
        out_shape=[jax.ShapeDtypeStruct((B, nt, D, T), BF16), out_tok,
                   jax.ShapeDtypeStruct((B, D, S), BF16), out_tok, out_tok, out_tok],
        compiler_params=pltpu.CompilerParams(
            dimension_semantics=("arbitrary", "arbitrary"),
            vmem_limit_bytes=_vmem_limit(resident=(2 * D * D + w_bf.size) * 2,
                                         streamed=tm * D * 4 + 6 * tm * D * 2,
                                         temporaries=8 * tm * D * 4)),
        name="inproj",
    )(x, mod3, row(norm1_g), wqT, wvT, w_bf, row(ln_v_g), row(ln_v_b),
      w_spatial, b_spatial.T)


def _attn_kernel(qT_ref, k_ref, vT_ref, tbl_ref, lam_ref, sg_ref, o_ref, va_ref):
    nt, _, T = qT_ref.shape
    n_heads = va_ref.shape[0]
    feat = lax.broadcasted_iota(jnp.int32, (V_DIM, T), 0)
    head_cols = [slice(hl * V_DIM, (hl + 1) * V_DIM) for hl in range(n_heads)]
    for hl in range(n_heads):
        va_ref[hl, 0:V_DIM, :] = vT_ref[head_cols[hl], :]
        va_ref[hl, V_DIM:, :] = jnp.ones((ONES_ROWS, va_ref.shape[2]), BF16)
    lam = (jnp.exp(jnp.sum(lam_ref[0:1, :] * lam_ref[1:2, :], axis=-1, keepdims=True))
           - jnp.exp(jnp.sum(lam_ref[2:3, :] * lam_ref[3:4, :], axis=-1, keepdims=True))
           + LAM_INIT)

    units = [(hl, i, mi) for i in reversed(range(nt)) for hl in range(n_heads) for mi in range(2)]
    tick = [0]
    s_done = {}
    p_done = {}
    outs = {}

    def key_chunks(i):
        near_lo, hi = max(i - 1, 0) * T, (i + 1) * T
        cuts = sorted(set(range(near_lo, -1, -KEY_CHUNK)) | {0})
        far = [(lo, nxt, None) for lo, nxt in zip(cuts[:-1], cuts[1:])]
        return far + [(near_lo, hi, slice(2 * T - (hi - near_lo), 2 * T))]

    def scores():
        for u in units:
            hl, i, mi = u
            in_map = (feat < HEAD_DIM) if mi == 0 else (feat >= HEAD_DIM)
            qz = jnp.where(in_map, qT_ref[i, head_cols[hl], :].astype(F32), 0.0).astype(BF16)
            for c, (lo, hi, tbl_rows) in enumerate(key_chunks(i)):
                s = jnp.dot(k_ref[lo:hi, head_cols[hl]], qz, preferred_element_type=F32)
                if tbl_rows is not None:
                    s = s + tbl_ref[hl, tbl_rows, :]
                s_done[u, c] = (s, tick[0])
                yield True

    def probs():
        for u in units:
            for c in range(len(key_chunks(u[1]))):
                while (u, c) not in s_done or tick[0] < s_done[u, c][1] + EXP_LAG:
                    yield False
                s, _ = s_done.pop((u, c))
                mc = jnp.max(s, axis=0, keepdims=True)
                p_done[u, c] = (jnp.exp2((s - mc).astype(BF16)), mc, tick[0])
                yield True

    def values():
        for u in units:
            hl, i, mi = u
            parts = []
            for c, (lo, hi, _) in enumerate(key_chunks(i)):
                while (u, c) not in p_done or tick[0] < p_done[u, c][2] + VALUE_LAG:
                    yield False
                p, mc, _ = p_done.pop((u, c))
                parts.append((jnp.dot(va_ref[hl, :, lo:hi], p, preferred_element_type=F32), mc))
                yield True
            m = functools.reduce(jnp.maximum, [mc for _, mc in parts])
            acc = functools.reduce(lambda a, b: a + b, [d * jnp.exp2(mc - m) for d, mc in parts])
            outs[u] = acc[0:V_DIM, :] * (1.0 / acc[V_DIM:V_DIM + 1, :])
            if mi == 1:
                oT = outs.pop((hl, i, 0)) - lam * outs.pop((hl, i, 1))
                yT = oT * lax.rsqrt(jnp.mean(oT * oT, axis=0, keepdims=True) + RMS_EPS)
                y = yT.T * sg_ref[...] * (1.0 - LAM_INIT)
                o_ref[i * T:(i + 1) * T, head_cols[hl]] = y.astype(BF16)

    streams = [scores(), probs(), values()]
    while streams:
        streams = [g for g in streams if next(g, None) is not None]
        tick[0] += 1


def _attention(qT, k, vT, tbl, lam_vecs, subln_g):
    B, nt, D, T = qT.shape
    S = nt * T
    H = D // V_DIM
    hp = ATTN_HEADS_PER_STEP
    W = hp * V_DIM
    return pl.pallas_call(
        _attn_kernel,
        grid=(B, H // hp),
        in_specs=[pl.BlockSpec((None, nt, W, T), lambda b, h: (b, 0, h, 0)),
                  pl.BlockSpec((None, S, W), lambda b, h: (b, 0, h)),
                  pl.BlockSpec((None, W, S), lambda b, h: (b, h, 0)),
                  pl.BlockSpec((hp, 2 * T, T), lambda b, h: (h, 0, 0)),
                  pl.BlockSpec((4, HEAD_DIM), lambda b, h: (0, 0)),
                  pl.BlockSpec((1, V_DIM), lambda b, h: (0, 0))],
        out_specs=pl.BlockSpec((None, S, W), lambda b, h: (b, 0, h)),
        out_shape=jax.ShapeDtypeStruct((B, S, D), BF16),
        scratch_shapes=[pltpu.VMEM((hp, V_DIM + ONES_ROWS, S), BF16)],
        compiler_params=pltpu.CompilerParams(
            dimension_semantics=("arbitrary", "arbitrary"),
            vmem_limit_bytes=_vmem_limit(resident=hp * (V_DIM + ONES_ROWS) * S * 2,
                                         streamed=4 * S * W * 2 + hp * 2 * T * T * 4,
                                         temporaries=ATTN_TEMP_BYTES)),
        name="attn",
    )(qT, k, vT, tbl, lam_vecs, subln_g.reshape(1, V_DIM))


def _tail_kernel(x_ref, oa_ref, ob_ref, sga_ref, sgb_ref, mod_ref, n2g_ref, fg_ref,
                 wpa_ref, wpb_ref, wo_ref, wfi_ref, wfo_ref, o_ref):
    hidden = wfo_ref.shape[0]
    tm = x_ref.shape[0]
    bounds = list(range(0, hidden, TAIL_HIDDEN_CHUNK)) + [hidden]

    def row_group(rows):
        pa = jnp.dot(oa_ref[rows, :], wpa_ref[...], preferred_element_type=F32)
        pb = jnp.dot(ob_ref[rows, :], wpb_ref[...], preferred_element_type=F32)
        yield True
        merged = (sga_ref[rows, :].astype(F32) * pa + sgb_ref[rows, :].astype(F32) * pb)
        x1 = x_ref[rows, :] + mod_ref[2:3, :] * jnp.dot(merged.astype(BF16), wo_ref[...],
                                                        preferred_element_type=F32)
        yield True
        y = x1 * lax.rsqrt(jnp.mean(x1 * x1, axis=-1, keepdims=True) + RMS_EPS) * n2g_ref[...]
        h2 = (y * (1.0 + mod_ref[4:5, :]) + mod_ref[3:4, :]).astype(BF16)
        ffn = None
        for lo, hi in zip(bounds[:-1], bounds[1:]):
            gate = jnp.dot(h2, wfi_ref[:, lo:hi], preferred_element_type=F32)
            up = jnp.dot(h2, wfi_ref[:, hidden + lo:hidden + hi], preferred_element_type=F32)
            yield True
            act = (gate * _sigmoid(gate) * up).astype(BF16)
            part = jnp.dot(act, wfo_ref[lo:hi, :], preferred_element_type=F32)
            ffn = part if ffn is None else ffn + part
        yield True
        x2 = x1 + mod_ref[5:6, :] * ffn
        o_ref[rows, :] = (x2 * lax.rsqrt(jnp.mean(x2 * x2, axis=-1, keepdims=True) + RMS_EPS)
                          * fg_ref[...])

    live = [row_group(slice(r, r + TAIL_ROWS)) for r in range(0, tm, TAIL_ROWS)]
    while live:
        live = [g for g in live if next(g, None) is not None]


def _tail(x, oa, ob, sga, sgb, mod3, norm2_g, final_g, w_proj_a, w_proj_b, w_out, w_ffn_in,
          w_ffn_out):
    B, S, D = x.shape
    T = TAIL_TILE
    hidden = w_ffn_out.shape[0]
    row = lambda a: a.reshape(1, D)
    tok = pl.BlockSpec((None, T, D), lambda b, t: (b, t, 0))
    return pl.pallas_call(
        _tail_kernel,
        grid=(B, S // T),
        in_specs=[tok, tok, tok, tok, tok,
                  pl.BlockSpec((None, 6, D), lambda b, t: (b, 0, 0)),
                  _resident((1, D)), _resident((1, D)),
                  _resident((D, D)), _resident((D, D)), _resident((D, D)),
                  _resident((D, 2 * hidden)), _resident((hidden, D))],
        out_specs=tok,
        out_shape=jax.ShapeDtypeStruct((B, S, D), F32),
        compiler_params=pltpu.CompilerParams(
            dimension_semantics=("arbitrary", "arbitrary"),
            vmem_limit_bytes=_vmem_limit(
                resident=(3 * D * D + 3 * D * hidden) * 2,
                streamed=2 * T * D * 4 + 4 * T * D * 2,
                temporaries=(T // TAIL_ROWS) * TAIL_ROWS * (6 * D + 2 * TAIL_HIDDEN_CHUNK) * 4)),
        name="tail",
    )(x, oa, ob, sga, sgb, mod3, row(norm2_g), row(final_g),
      w_proj_a.astype(BF16), w_proj_b.astype(BF16), w_out.astype(BF16),
      w_ffn_in.astype(BF16), w_ffn_out.astype(BF16))


def kernel(x, c, w_ada, b_ada, norm1_g, norm2_g, w_in, lambda_q1, lambda_k1, lambda_q2, lambda_k2,
           subln_g, ln_v_g, ln_v_b, w_spatial, b_spatial, w_proj_a, w_proj_b, w_out, w_ffn_in,
           w_ffn_out, rel_bias, final_g):
    B, S, D = x.shape
    assert w_ada.shape[0] == 1, "single-layer block"
    assert S % INPROJ_TILE == 0 and S % TAIL_TILE == 0
    assert INPROJ_TILE % INPROJ_ROWS == 0 and INPROJ_ROWS % SEQ_TILE == 0
    assert INPROJ_ROWS % CHUNK == 0 and TAIL_TILE % TAIL_ROWS == 0
    assert SEQ_TILE >= REL_MAX_DIST, "bias must be constant beyond the first off-diagonal tile"

    mod3 = _ada(c, w_ada[0], b_ada[0]).reshape(B, 6, D)
    tbl = _bias_tiles(rel_bias, SEQ_TILE)
    qT, k, vT, ob, sga, sgb = _inproj(x, mod3, norm1_g[0], w_in[0], ln_v_g[0], ln_v_b[0],
                                      w_spatial[0], b_spatial[0])
    lam_vecs = jnp.concatenate([lambda_q1, lambda_k1, lambda_q2, lambda_k2], axis=0)
    oa = _attention(qT, k, vT, tbl, lam_vecs, subln_g[0])
    return _tail(x, oa, ob, sga, sgb, mod3, norm2_g[0], final_g, w_proj_a[0], w_proj_b[0],
                 w_out[0], w_ffn_in[0], w_ffn_out[0])
```

```python
import functools
import math

import numpy as np
import jax
import jax.numpy as jnp
from jax import lax
from jax.experimental import pallas as pl
from jax.experimental.pallas import tpu as pltpu

F32 = jnp.float32
BF16 = jnp.bfloat16

RMS_EPS = 1e-6
LN_EPS = 1e-5
HEAD_DIM = 64
V_DIM = 2 * HEAD_DIM
CHUNK = 128
GROUP_WIDTH = 128
REL_BUCKETS = 32
REL_MAX_EXACT = REL_BUCKETS // 2
REL_MAX_DIST = 128
LAM_INIT = 0.8 - 0.6 * math.exp(-0.3 * 0)
MASK_VALUE = -1e30
LOG2E = math.log2(math.e)
ONES_ROWS = 16
ATTN_HEADS_PER_STEP = 2
KEY_CHUNK = 256
EXP_LAG = 8
VALUE_LAG = 2

SEQ_TILE = 256
INPROJ_TILE = 512
INPROJ_ROWS = 512
INPROJ_SLAB = 128
INPROJ_EARLY_CHUNKS = 2
INPROJ_COLS = 256
TAIL_TILE = 512
TAIL_ROWS = 256
TAIL_HIDDEN_CHUNK = 1536
V7X_VMEM_BYTES = 64 * 1024 * 1024
VMEM_HEADROOM = 6 * 1024 * 1024
ATTN_TEMP_BYTES = 46 * 1024 * 1024

_NT = (((1,), (1,)), ((), ()))
SEC_Q, SEC_K, SEC_V, SEC_U, SEC_G, SEC_GA, SEC_GB = range(7)


def _sigmoid(x):
    return 1.0 / (1.0 + jnp.exp2(x * (-LOG2E)))


def _gelu_tanh(x):
    a = -2.0 * math.sqrt(2.0 / math.pi) * LOG2E
    return x / (1.0 + jnp.exp2(x * (a + (a * 0.044715) * (x * x))))


def _vmem_limit(resident, streamed, temporaries):
    return int(min(resident + 2 * streamed + temporaries, V7X_VMEM_BYTES - VMEM_HEADROOM))


def _resident(shape):
    n = len(shape)
    return pl.BlockSpec(shape, lambda *_: (0,) * n, pipeline_mode=pl.Buffered(1))


def _ada_kernel(c_ref, w_ref, b_ref, o_ref):
    c = c_ref[...]
    ca = (c * _sigmoid(c)).astype(BF16)
    o_ref[...] = jnp.dot(ca, w_ref[...].astype(BF16), preferred_element_type=F32) + b_ref[...]


def _ada(c, w_ada, b_ada):
    B, D = c.shape
    N = w_ada.shape[1]
    tn = 1024
    return pl.pallas_call(
        _ada_kernel,
        grid=(N // tn,),
        in_specs=[pl.BlockSpec((B, D), lambda n: (0, 0)),
                  pl.BlockSpec((D, tn), lambda n: (0, n)),
                  pl.BlockSpec((1, tn), lambda n: (0, n))],
        out_specs=pl.BlockSpec((B, tn), lambda n: (0, n)),
        out_shape=jax.ShapeDtypeStruct((B, N), F32),
        name="ada",
    )(c, w_ada, b_ada.reshape(1, N))


def _bucket_tiles(T):
    kk = np.arange(T, dtype=np.int32)[:, None]
    qq = np.arange(T, dtype=np.int32)[None, :]
    out = []
    for o in (1, 0):
        dist = o * T + qq - kk
        n = np.maximum(dist, 0)
        nf = np.maximum(n, 1).astype(np.float32)
        large = REL_MAX_EXACT + (np.log(nf / np.float32(REL_MAX_EXACT))
                                 / np.float32(math.log(REL_MAX_DIST / REL_MAX_EXACT))
                                 * np.float32(REL_BUCKETS - REL_MAX_EXACT)).astype(np.int32)
        large = np.minimum(large, REL_BUCKETS - 1)
        b = np.where(n < REL_MAX_EXACT, n, large)
        out.append(np.where(dist < 0, -1, b).astype(np.int32))
    return np.concatenate(out, axis=0)


def _bias_kernel(rb_ref, bucket_ref, o_ref):
    h = pl.program_id(0)
    bk = bucket_ref[...]
    far = rb_ref[REL_BUCKETS - 1, h]
    acc = jnp.where(bk < 0, MASK_VALUE, 0.0).astype(F32)
    for b in range(REL_BUCKETS - 1):
        acc = jnp.where(bk == b, (rb_ref[b, h] - far) * LOG2E, acc)
    o_ref[...] = acc


def _bias_tiles(rel_bias, T):
    H = rel_bias.shape[1]
    buckets = jnp.asarray(_bucket_tiles(T))
    return pl.pallas_call(
        _bias_kernel,
        grid=(H,),
        in_specs=[pl.BlockSpec(memory_space=pltpu.SMEM),
                  pl.BlockSpec((2 * T, T), lambda h: (0, 0))],
        out_specs=pl.BlockSpec((None, 2 * T, T), lambda h: (h, 0, 0)),
        out_shape=jax.ShapeDtypeStruct((H, 2 * T, T), F32),
        name="bias_tiles",
    )(rel_bias, buckets)


def _inproj_kernel(x_ref, mod_ref, n1g_ref, wqT_ref, wvT_ref, w_ref, lng_ref, lnb_ref, ws_ref, bs_ref,
                   qT_ref, k_ref, vT_ref, ob_ref, sga_ref, sgb_ref):
    tm, D = x_ref.shape
    NC = INPROJ_COLS
    T = qT_ref.shape[-1]
    chunks = [slice(c * NC, (c + 1) * NC) for c in range(D // NC)]

    rnd = [0]
    queue = []

    def put(fn, *args):
        queue.append((rnd[0], functools.partial(fn, *args)))

    R = INPROJ_ROWS

    def row_group(j):
        rs = slice(j * R, (j + 1) * R)
        g_parts, u_parts, v_parts, stats = {}, {}, {}, {}

        def normed(rows):
            x = x_ref[rs.start + rows.start:rs.start + rows.stop, :]
            y = x * lax.rsqrt(jnp.mean(x * x, axis=-1, keepdims=True) + RMS_EPS) * n1g_ref[...]
            return (y * (1.0 + mod_ref[1:2, :]) + mod_ref[0:1, :]).astype(BF16)

        def epi_g(c, z):
            g = _gelu_tanh(z)
            g_parts[c] = g
            rsum = jnp.sum(g, axis=-1, keepdims=True)
            stats["sum"] = rsum if c == 0 else stats["sum"] + rsum
            if c == len(chunks) - 1:
                for cc in range(len(chunks)):
                    put(ln_center, cc)
                for cc in range(len(chunks)):
                    put(ln_scale, cc)

        def ln_center(c):
            gc = g_parts[c] - stats["sum"] * (1.0 / D)
            g_parts[c] = gc
            sq = jnp.sum(gc * gc, axis=-1, keepdims=True)
            stats["sq"] = sq if c == 0 else stats["sq"] + sq

        def ln_scale(c):
            rstd = lax.rsqrt(stats["sq"] * (1.0 / D) + LN_EPS)
            v = g_parts.pop(c) * rstd * lng_ref[:, chunks[c]] + lnb_ref[:, chunks[c]]
            v_parts[c] = v.astype(BF16)

        def epi_u(c, z):
            u_parts[c] = _gelu_tanh(z)

        def epi_k(c, z):
            k_ref[rs, chunks[c]] = z.astype(BF16)

        def epi_q(c, z):
            qz = (z * (LOG2E / math.sqrt(HEAD_DIM))).astype(BF16)
            for jj in range(R // T):
                qT_ref[j * (R // T) + jj, chunks[c], :] = qz[:, jj * T:(jj + 1) * T]

        def epi_v(c, z):
            vT_ref[chunks[c], rs] = z.astype(BF16)

        def epi_gate(ref, c, z):
            ref[rs, chunks[c]] = _sigmoid(z).astype(BF16)

        chunk_rows = [slice(ci * CHUNK, (ci + 1) * CHUNK) for ci in range(R // CHUNK)]

        def group_cols(gi):
            lo = gi * GROUP_WIDTH
            return lo // NC, slice(lo % NC, lo % NC + GROUP_WIDTH)

        def epi_sv(gi, sv):
            c, local = group_cols(gi)
            cols = slice(gi * GROUP_WIDTH, (gi + 1) * GROUP_WIDTH)
            for ci, rows in enumerate(chunk_rows):
                sv_c = sv[:, ci * GROUP_WIDTH:(ci + 1) * GROUP_WIDTH]
                out_rows = slice(j * R + rows.start, j * R + rows.stop)
                ob_ref[out_rows, cols] = (u_parts[c][rows, local] * sv_c).astype(BF16)

        def rows_dot(sec, cols, lhs=None):
            return jnp.dot(h if lhs is None else lhs,
                           w_ref[:, sec * D + cols.start:sec * D + cols.stop],
                           preferred_element_type=F32)

        def feat_dot(wT_ref, rows):
            return lax.dot_general(wT_ref[rows, :], h, _NT, preferred_element_type=F32)

        heavy = ([(epi_g, (c,), rows_dot, (SEC_G, s)) for c, s in enumerate(chunks)]
                 + [(epi_u, (c,), rows_dot, (SEC_U, s)) for c, s in enumerate(chunks)]
                 + [(epi_gate, (sga_ref, c), rows_dot, (SEC_GA, s)) for c, s in enumerate(chunks)]
                 + [(epi_gate, (sgb_ref, c), rows_dot, (SEC_GB, s)) for c, s in enumerate(chunks)])
        light = ([(epi_k, (c,), rows_dot, (SEC_K, s)) for c, s in enumerate(chunks)]
                 + [(epi_q, (c,), feat_dot, (wqT_ref, s)) for c, s in enumerate(chunks)]
                 + [(epi_v, (c,), feat_dot, (wvT_ref, s)) for c, s in enumerate(chunks)])
        order = []
        while heavy or light:
            order += heavy[:1] + light[:1]
            heavy, light = heavy[1:], light[1:]
        tri = (lax.broadcasted_iota(jnp.int32, (CHUNK, CHUNK), 0)
               >= lax.broadcasted_iota(jnp.int32, (CHUNK, CHUNK), 1))

        def spatial(gi):
            ws = jnp.where(tri, ws_ref[gi], 0.0).astype(BF16)
            c, local = group_cols(gi)
            vcat = jnp.concatenate([v_parts[c][rows, local] for rows in chunk_rows], axis=1)
            sv = jnp.dot(ws, vcat, preferred_element_type=F32) + bs_ref[:, gi:gi + 1]
            put(epi_sv, gi, sv)

        early, order = order[:INPROJ_EARLY_CHUNKS], order[INPROJ_EARLY_CHUNKS:]
        assert all(dot_fn is rows_dot for _, _, dot_fn, _ in early)
        h_slabs, z_slabs = [], [[] for _ in early]
        for lo in range(0, R, INPROJ_SLAB):
            h_slabs.append(normed(slice(lo, lo + INPROJ_SLAB)))
            for zs, (_, _, _, dot_args) in zip(z_slabs, early):
                zs.append(rows_dot(*dot_args, lhs=h_slabs[-1]))
        h = jnp.concatenate(h_slabs, axis=0)
        for zs, (epi, epi_args, _, _) in zip(z_slabs, early):
            put(epi, *epi_args, jnp.concatenate(zs, axis=0))
        yield True

        spatial_todo = list(range(ws_ref.shape[0]))
        for epi, epi_args, dot_fn, dot_args in order:
            put(epi, *epi_args, dot_fn(*dot_args))
            yield True
            if spatial_todo and len(v_parts) == len(chunks) and len(u_parts) == len(chunks):
                spatial(spatial_todo.pop(0))
                yield True
        for gi in spatial_todo:
            spatial(gi)
            yield True

    live = [row_group(j) for j in range(tm // R)]
    while live or queue:
        live = [g for g in live if next(g, None) is not None]
        rnd[0] += 1
        ready = sum(1 for r, _ in queue if r < rnd[0] - 1) if live else len(queue)
        for _ in range(min(ready, 2 * len(live) if live else ready)):
            queue.pop(0)[1]()


def _inproj(x, mod3, norm1_g, w_in, ln_v_g, ln_v_b, w_spatial, b_spatial):
    B, S, D = x.shape
    T = SEQ_TILE
    nt = S // T
    tm = INPROJ_TILE
    w_bf = w_in.astype(BF16)
    wqT = w_bf[:, SEC_Q * D:(SEC_Q + 1) * D].T
    wvT = w_bf[:, SEC_V * D:(SEC_V + 1) * D].T
    G = w_spatial.shape[0]
    row = lambda a: a.reshape(1, D)
    tok = pl.BlockSpec((None, tm, D), lambda b, t: (b, t, 0))
    q_featT = pl.BlockSpec((None, tm // T, D, T), lambda b, t: (b, t, 0, 0))
    v_featT = pl.BlockSpec((None, D, tm), lambda b, t: (b, 0, t))
    out_tok = jax.ShapeDtypeStruct((B, S, D), BF16)
    return pl.pallas_call(
        _inproj_kernel,
        grid=(B, S // tm),
        in_specs=[tok,
                  pl.BlockSpec((None, 6, D), lambda b, t: (b, 0, 0)),
                  _resident((1, D)),
                  _resident((D, D)), _resident((D, D)), _resident(w_bf.shape),
                  _resident((1, D)), _resident((1, D)),
                  _resident((G, CHUNK, CHUNK)), _resident((CHUNK, G))],
        out_specs=[q_featT, tok, v_featT, tok, tok, tok],
        out_shape=[jax.ShapeDtypeStruct((B, nt, D, T), BF16), out_tok,
                   jax.ShapeDtypeStruct((B, D, S), BF16), out_tok, out_tok, out_tok],
        compiler_params=pltpu.CompilerParams(
            dimension_semantics=("arbitrary", "arbitrary"),
            vmem_limit_bytes=_vmem_limit(resident=(2 * D * D + w_bf.size) * 2,
                                         streamed=tm * D * 4 + 6 * tm * D * 2,
                                         temporaries=8 * tm * D * 4)),
        name="inproj",
    )(x, mod3, row(norm1_g), wqT, wvT, w_bf, row(ln_v_g), row(ln_v_b),
      w_spatial, b_spatial.T)


def _attn_kernel(qT_ref, k_ref, vT_ref, tbl_ref, lam_ref, sg_ref, o_ref, va_ref):
    nt, _, T = qT_ref.shape
    n_heads = va_ref.shape[0]
    feat = lax.broadcasted_iota(jnp.int32, (V_DIM, T), 0)
    head_cols = [slice(hl * V_DIM, (hl + 1) * V_DIM) for hl in range(n_heads)]
    for hl in range(n_heads):
        va_ref[hl, 0:V_DIM, :] = vT_ref[head_cols[hl], :]
        va_ref[hl, V_DIM:, :] = jnp.ones((ONES_ROWS, va_ref.shape[2]), BF16)
    lam = (jnp.exp(jnp.sum(lam_ref[0:1, :] * lam_ref[1:2, :], axis=-1, keepdims=True))
           - jnp.exp(jnp.sum(lam_ref[2:3, :] * lam_ref[3:4, :], axis=-1, keepdims=True))
           + LAM_INIT)

    units = [(hl, i, mi) for i in reversed(range(nt)) for hl in range(n_heads) for mi in range(2)]
    tick = [0]
    s_done = {}
    p_done = {}
    outs = {}

    def key_chunks(i):
        near_lo, hi = max(i - 1, 0) * T, (i + 1) * T
        cuts = sorted(set(range(near_lo, -1, -KEY_CHUNK)) | {0})
        far = [(lo, nxt, None) for lo, nxt in zip(cuts[:-1], cuts[1:])]
        return far + [(near_lo, hi, slice(2 * T - (hi - near_lo), 2 * T))]

    def scores():
        for u in units:
            hl, i, mi = u
            in_map = (feat < HEAD_DIM) if mi == 0 else (feat >= HEAD_DIM)
            qz = jnp.where(in_map, qT_ref[i, head_cols[hl], :].astype(F32), 0.0).astype(BF16)
            for c, (lo, hi, tbl_rows) in enumerate(key_chunks(i)):
                s = jnp.dot(k_ref[lo:hi, head_cols[hl]], qz, preferred_element_type=F32)
                if tbl_rows is not None:
                    s = s + tbl_ref[hl, tbl_rows, :]
                s_done[u, c] = (s, tick[0])
                yield True

    def probs():
        for u in units:
            for c in range(len(key_chunks(u[1]))):
                while (u, c) not in s_done or tick[0] < s_done[u, c][1] + EXP_LAG:
                    yield False
                s, _ = s_done.pop((u, c))
                mc = jnp.max(s, axis=0, keepdims=True)
                p_done[u, c] = (jnp.exp2((s - mc).astype(BF16)), mc, tick[0])
                yield True

    def values():
        for u in units:
            hl, i, mi = u
            parts = []
            for c, (lo, hi, _) in enumerate(key_chunks(i)):
                while (u, c) not in p_done or tick[0] < p_done[u, c][2] + VALUE_LAG:
                    yield False
                p, mc, _ = p_done.pop((u, c))
                parts.append((jnp.dot(va_ref[hl, :, lo:hi], p, preferred_element_type=F32), mc))
                yield True
            m = functools.reduce(jnp.maximum, [mc for _, mc in parts])
            acc = functools.reduce(lambda a, b: a + b, [d * jnp.exp2(mc - m) for d, mc in parts])
            outs[u] = acc[0:V_DIM, :] * (1.0 / acc[V_DIM:V_DIM + 1, :])
            if mi == 1:
                oT = outs.pop((hl, i, 0)) - lam * outs.pop((hl, i, 1))
                yT = oT * lax.rsqrt(jnp.mean(oT * oT, axis=0, keepdims=True) + RMS_EPS)
                y = yT.T * sg_ref[...] * (1.0 - LAM_INIT)
                o_ref[i * T:(i + 1) * T, head_cols[hl]] = y.astype(BF16)

    streams = [scores(), probs(), values()]
    while streams:
        streams = [g for g in streams if next(g, None) is not None]
        tick[0] += 1


def _attention(qT, k, vT, tbl, lam_vecs, subln_g):
    B, nt, D, T = qT.shape
    S = nt * T
    H = D // V_DIM
    hp = ATTN_HEADS_PER_STEP
    W = hp * V_DIM
    return pl.pallas_call(
        _attn_kernel,
        grid=(B, H // hp),
        in_specs=[pl.BlockSpec((None, nt, W, T), lambda b, h: (b, 0, h, 0)),
                  pl.BlockSpec((None, S, W), lambda b, h: (b, 0, h)),
                  pl.BlockSpec((None, W, S), lambda b, h: (b, h, 0)),
                  pl.BlockSpec((hp, 2 * T, T), lambda b, h: (h, 0, 0)),
                  pl.BlockSpec((4, HEAD_DIM), lambda b, h: (0, 0)),
                  pl.BlockSpec((1, V_DIM), lambda b, h: (0, 0))],
        out_specs=pl.BlockSpec((None, S, W), lambda b, h: (b, 0, h)),
        out_shape=jax.ShapeDtypeStruct((B, S, D), BF16),
        scratch_shapes=[pltpu.VMEM((hp, V_DIM + ONES_ROWS, S), BF16)],
        compiler_params=pltpu.CompilerParams(
            dimension_semantics=("arbitrary", "arbitrary"),
            vmem_limit_bytes=_vmem_limit(resident=hp * (V_DIM + ONES_ROWS) * S * 2,
                                         streamed=4 * S * W * 2 + hp * 2 * T * T * 4,
                                         temporaries=ATTN_TEMP_BYTES)),
        name="attn",
    )(qT, k, vT, tbl, lam_vecs, subln_g.reshape(1, V_DIM))


def _tail_kernel(x_ref, oa_ref, ob_ref, sga_ref, sgb_ref, mod_ref, n2g_ref, fg_ref,
                 wpa_ref, wpb_ref, wo_ref, wfi_ref, wfo_ref, o_ref):
    hidden = wfo_ref.shape[0]
    tm = x_ref.shape[0]
    bounds = list(range(0, hidden, TAIL_HIDDEN_CHUNK)) + [hidden]

    def row_group(rows):
        pa = jnp.dot(oa_ref[rows, :], wpa_ref[...], preferred_element_type=F32)
        pb = jnp.dot(ob_ref[rows, :], wpb_ref[...], preferred_element_type=F32)
        yield True
        merged = (sga_ref[rows, :].astype(F32) * pa + sgb_ref[rows, :].astype(F32) * pb)
        x1 = x_ref[rows, :] + mod_ref[2:3, :] * jnp.dot(merged.astype(BF16), wo_ref[...],
                                                        preferred_element_type=F32)
        yield True
        y = x1 * lax.rsqrt(jnp.mean(x1 * x1, axis=-1, keepdims=True) + RMS_EPS) * n2g_ref[...]
        h2 = (y * (1.0 + mod_ref[4:5, :]) + mod_ref[3:4, :]).astype(BF16)
        ffn = None
        for lo, hi in zip(bounds[:-1], bounds[1:]):
            gate = jnp.dot(h2, wfi_ref[:, lo:hi], preferred_element_type=F32)
            up = jnp.dot(h2, wfi_ref[:, hidden + lo:hidden + hi], preferred_element_type=F32)
            yield True
            act = (gate * _sigmoid(gate) * up).astype(BF16)
            part = jnp.dot(act, wfo_ref[lo:hi, :], preferred_element_type=F32)
            ffn = part if ffn is None else ffn + part
        yield True
        x2 = x1 + mod_ref[5:6, :] * ffn
        o_ref[rows, :] = (x2 * lax.rsqrt(jnp.mean(x2 * x2, axis=-1, keepdims=True) + RMS_EPS)
                          * fg_ref[...])

    live = [row_group(slice(r, r + TAIL_ROWS)) for r in range(0, tm, TAIL_ROWS)]
    while live:
        live = [g for g in live if next(g, None) is not None]


def _tail(x, oa, ob, sga, sgb, mod3, norm2_g, final_g, w_proj_a, w_proj_b, w_out, w_ffn_in,
          w_ffn_out):
    B, S, D = x.shape
    T = TAIL_TILE
    hidden = w_ffn_out.shape[0]
    row = lambda a: a.reshape(1, D)
    tok = pl.BlockSpec((None, T, D), lambda b, t: (b, t, 0))
    return pl.pallas_call(
        _tail_kernel,
        grid=(B, S // T),
        in_specs=[tok, tok, tok, tok, tok,
                  pl.BlockSpec((None, 6, D), lambda b, t: (b, 0, 0)),
                  _resident((1, D)), _resident((1, D)),
                  _resident((D, D)), _resident((D, D)), _resident((D, D)),
                  _resident((D, 2 * hidden)), _resident((hidden, D))],
        out_specs=tok,
        out_shape=jax.ShapeDtypeStruct((B, S, D), F32),
        compiler_params=pltpu.CompilerParams(
            dimension_semantics=("arbitrary", "arbitrary"),
            vmem_limit_bytes=_vmem_limit(
                resident=(3 * D * D + 3 * D * hidden) * 2,
                streamed=2 * T * D * 4 + 4 * T * D * 2,
                temporaries=(T // TAIL_ROWS) * TAIL_ROWS * (6 * D + 2 * TAIL_HIDDEN_CHUNK) * 4)),
        name="tail",
    )(x, oa, ob, sga, sgb, mod3, row(norm2_g), row(final_g),
      w_proj_a, w_proj_b, w_out, w_ffn_in, w_ffn_out)


def kernel(x, c, w_ada, b_ada, norm1_g, norm2_g, w_in, lambda_q1, lambda_k1, lambda_q2, lambda_k2,
           subln_g, ln_v_g, ln_v_b, w_spatial, b_spatial, w_proj_a, w_proj_b, w_out, w_ffn_in,
           w_ffn_out, rel_bias, final_g):
    B, S, D = x.shape
    assert w_ada.shape[0] == 1, "single-layer block"
    assert S % INPROJ_TILE == 0 and S % TAIL_TILE == 0
    assert INPROJ_TILE % INPROJ_ROWS == 0 and INPROJ_ROWS % SEQ_TILE == 0
    assert INPROJ_ROWS % CHUNK == 0 and TAIL_TILE % TAIL_ROWS == 0
    assert SEQ_TILE >= REL_MAX_DIST, "bias must be constant beyond the first off-diagonal tile"

    mod3 = _ada(c, w_ada[0], b_ada[0]).reshape(B, 6, D)
    tail_w = [w[0].astype(BF16) for w in (w_proj_a, w_proj_b, w_out, w_ffn_in, w_ffn_out)]
    mod3, tail_w = lax.optimization_barrier((mod3, tail_w))
    tbl = _bias_tiles(rel_bias, SEQ_TILE)
    qT, k, vT, ob, sga, sgb = _inproj(x, mod3, norm1_g[0], w_in[0], ln_v_g[0], ln_v_b[0],
                                      w_spatial[0], b_spatial[0])
    lam_vecs = jnp.concatenate([lambda_q1, lambda_k1, lambda_q2, lambda_k2], axis=0)
    oa = _attention(qT, k, vT, tbl, lam_vecs, subln_g[0])
    return _tail(x, oa, ob, sga, sgb, mod3, norm2_g[0], final_g, *tail_w)
```

```python
import functools
import math

import numpy as np
import jax
import jax.numpy as jnp
from jax import lax
from jax.experimental import pallas as pl
from jax.experimental.pallas import tpu as pltpu

F32 = jnp.float32
BF16 = jnp.bfloat16

RMS_EPS = 1e-6
LN_EPS = 1e-5
HEAD_DIM = 64
V_DIM = 2 * HEAD_DIM
CHUNK = 128
GROUP_WIDTH = 128
REL_BUCKETS = 32
REL_MAX_EXACT = REL_BUCKETS // 2
REL_MAX_DIST = 128
LAM_INIT = 0.8 - 0.6 * math.exp(-0.3 * 0)
MASK_VALUE = -1e30
LOG2E = math.log2(math.e)
ONES_ROWS = 16
ATTN_HEADS_PER_STEP = 2
KEY_CHUNK = 256
EXP_LAG = 8
VALUE_LAG = 2

SEQ_TILE = 256
INPROJ_TILE = 512
INPROJ_ROWS = 512
INPROJ_SLAB = 128
INPROJ_EARLY_CHUNKS = 2
INPROJ_COLS = 256
TAIL_TILE = 512
TAIL_ROWS = 256
TAIL_HIDDEN_CHUNK = 1536
V7X_VMEM_BYTES = 64 * 1024 * 1024
VMEM_HEADROOM = 6 * 1024 * 1024
ATTN_TEMP_BYTES = 46 * 1024 * 1024

_NT = (((1,), (1,)), ((), ()))
SEC_Q, SEC_K, SEC_V, SEC_U, SEC_G, SEC_GA, SEC_GB = range(7)


def _sigmoid(x):
    return 1.0 / (1.0 + jnp.exp2(x * (-LOG2E)))


def _gelu_tanh(x):
    a = -2.0 * math.sqrt(2.0 / math.pi) * LOG2E
    return x / (1.0 + jnp.exp2(x * (a + (a * 0.044715) * (x * x))))


def _vmem_limit(resident, streamed, temporaries):
    return int(min(resident + 2 * streamed + temporaries, V7X_VMEM_BYTES - VMEM_HEADROOM))


def _resident(shape):
    n = len(shape)
    return pl.BlockSpec(shape, lambda *_: (0,) * n, pipeline_mode=pl.Buffered(1))


def _ada_kernel(c_ref, w_ref, b_ref, o_ref):
    c = c_ref[...]
    ca = (c * _sigmoid(c)).astype(BF16)
    o_ref[...] = jnp.dot(ca, w_ref[...].astype(BF16), preferred_element_type=F32) + b_ref[...]


def _ada(c, w_ada, b_ada):
    B, D = c.shape
    N = w_ada.shape[1]
    tn = 1024
    return pl.pallas_call(
        _ada_kernel,
        grid=(N // tn,),
        in_specs=[pl.BlockSpec((B, D), lambda n: (0, 0)),
                  pl.BlockSpec((D, tn), lambda n: (0, n)),
                  pl.BlockSpec((1, tn), lambda n: (0, n))],
        out_specs=pl.BlockSpec((B, tn), lambda n: (0, n)),
        out_shape=jax.ShapeDtypeStruct((B, N), F32),
        name="ada",
    )(c, w_ada, b_ada.reshape(1, N))


def _bucket_tiles(T):
    kk = np.arange(T, dtype=np.int32)[:, None]
    qq = np.arange(T, dtype=np.int32)[None, :]
    out = []
    for o in (1, 0):
        dist = o * T + qq - kk
        n = np.maximum(dist, 0)
        nf = np.maximum(n, 1).astype(np.float32)
        large = REL_MAX_EXACT + (np.log(nf / np.float32(REL_MAX_EXACT))
                                 / np.float32(math.log(REL_MAX_DIST / REL_MAX_EXACT))
                                 * np.float32(REL_BUCKETS - REL_MAX_EXACT)).astype(np.int32)
        large = np.minimum(large, REL_BUCKETS - 1)
        b = np.where(n < REL_MAX_EXACT, n, large)
        out.append(np.where(dist < 0, -1, b).astype(np.int32))
    return np.concatenate(out, axis=0)


def _bias_kernel(rb_ref, bucket_ref, o_ref, *, layout):
    h = pl.program_id(0)
    R = bucket_ref.shape[-1]
    far = rb_ref[REL_BUCKETS - 1, h]
    blocks = []
    for n in range(bucket_ref.shape[0]):
        bk = bucket_ref[n]
        acc = jnp.where(bk < 0, MASK_VALUE, 0.0).astype(F32)
        for b in range(REL_BUCKETS - 1):
            acc = jnp.where(bk == b, (rb_ref[b, h] - far) * LOG2E, acc)
        blocks.append(acc)
    for r, c, kind in layout:
        if kind == "far":
            blk = jnp.zeros((R, R), F32)
        elif kind == "masked":
            blk = jnp.full((R, R), MASK_VALUE, F32)
        else:
            blk = blocks[kind]
        o_ref[r:r + R, c:c + R] = blk


def _bias_tiles(rel_bias, T):
    H = rel_bias.shape[1]
    R = REL_MAX_DIST
    assert T % R == 0
    full = _bucket_tiles(T)
    distinct, index, layout = [], {}, []
    for r in range(0, 2 * T, R):
        for c in range(0, T, R):
            blk = full[r:r + R, c:c + R]
            if (blk == REL_BUCKETS - 1).all():
                kind = "far"
            elif (blk == -1).all():
                kind = "masked"
            else:
                kind = index.setdefault(blk.tobytes(), len(distinct))
                if kind == len(distinct):
                    distinct.append(blk)
            layout.append((r, c, kind))
    buckets = jnp.asarray(np.stack(distinct))
    return pl.pallas_call(
        functools.partial(_bias_kernel, layout=layout),
        grid=(H,),
        in_specs=[pl.BlockSpec(memory_space=pltpu.SMEM),
                  pl.BlockSpec(buckets.shape, lambda h: (0, 0, 0))],
        out_specs=pl.BlockSpec((None, 2 * T, T), lambda h: (h, 0, 0)),
        out_shape=jax.ShapeDtypeStruct((H, 2 * T, T), F32),
        name="bias_tiles",
    )(rel_bias, buckets)


def _inproj_kernel(x_ref, mod_ref, n1g_ref, wqT_ref, wvT_ref, w_ref, lng_ref, lnb_ref, ws_ref, bs_ref,
                   qT_ref, k_ref, vT_ref, ob_ref, sga_ref, sgb_ref):
    tm, D = x_ref.shape
    NC = INPROJ_COLS
    T = qT_ref.shape[-1]
    chunks = [slice(c * NC, (c + 1) * NC) for c in range(D // NC)]

    rnd = [0]
    queue = []

    def put(fn, *args):
        queue.append((rnd[0], functools.partial(fn, *args)))

    R = INPROJ_ROWS

    def row_group(j):
        rs = slice(j * R, (j + 1) * R)
        g_parts, u_parts, v_parts, stats = {}, {}, {}, {}

        def normed(rows):
            x = x_ref[rs.start + rows.start:rs.start + rows.stop, :]
            y = x * lax.rsqrt(jnp.mean(x * x, axis=-1, keepdims=True) + RMS_EPS) * n1g_ref[...]
            return (y * (1.0 + mod_ref[1:2, :]) + mod_ref[0:1, :]).astype(BF16)

        def epi_g(c, z):
            g = _gelu_tanh(z)
            g_parts[c] = g
            rsum = jnp.sum(g, axis=-1, keepdims=True)
            stats["sum"] = rsum if c == 0 else stats["sum"] + rsum
            if c == len(chunks) - 1:
                for cc in range(len(chunks)):
                    put(ln_center, cc)
                for cc in range(len(chunks)):
                    put(ln_scale, cc)

        def ln_center(c):
            gc = g_parts[c] - stats["sum"] * (1.0 / D)
            g_parts[c] = gc
            sq = jnp.sum(gc * gc, axis=-1, keepdims=True)
            stats["sq"] = sq if c == 0 else stats["sq"] + sq

        def ln_scale(c):
            rstd = lax.rsqrt(stats["sq"] * (1.0 / D) + LN_EPS)
            v = g_parts.pop(c) * rstd * lng_ref[:, chunks[c]] + lnb_ref[:, chunks[c]]
            v_parts[c] = v.astype(BF16)

        def epi_u(c, z):
            u_parts[c] = _gelu_tanh(z)

        def epi_k(c, z):
            k_ref[rs, chunks[c]] = z.astype(BF16)

        def epi_q(c, z):
            qz = (z * (LOG2E / math.sqrt(HEAD_DIM))).astype(BF16)
            for jj in range(R // T):
                qT_ref[j * (R // T) + jj, chunks[c], :] = qz[:, jj * T:(jj + 1) * T]

        def epi_v(c, z):
            vT_ref[chunks[c], rs] = z.astype(BF16)

        def epi_gate(ref, c, z):
            ref[rs, chunks[c]] = _sigmoid(z).astype(BF16)

        chunk_rows = [slice(ci * CHUNK, (ci + 1) * CHUNK) for ci in range(R // CHUNK)]

        def group_cols(gi):
            lo = gi * GROUP_WIDTH
            return lo // NC, slice(lo % NC, lo % NC + GROUP_WIDTH)

        def epi_sv(gi, sv):
            c, local = group_cols(gi)
            cols = slice(gi * GROUP_WIDTH, (gi + 1) * GROUP_WIDTH)
            for ci, rows in enumerate(chunk_rows):
                sv_c = sv[:, ci * GROUP_WIDTH:(ci + 1) * GROUP_WIDTH]
                out_rows = slice(j * R + rows.start, j * R + rows.stop)
                ob_ref[out_rows, cols] = (u_parts[c][rows, local] * sv_c).astype(BF16)

        def rows_dot(sec, cols, lhs=None):
            return jnp.dot(h if lhs is None else lhs,
                           w_ref[:, sec * D + cols.start:sec * D + cols.stop],
                           preferred_element_type=F32)

        def feat_dot(wT_ref, rows):
            return lax.dot_general(wT_ref[rows, :], h, _NT, preferred_element_type=F32)

        heavy = ([(epi_g, (c,), rows_dot, (SEC_G, s)) for c, s in enumerate(chunks)]
                 + [(epi_u, (c,), rows_dot, (SEC_U, s)) for c, s in enumerate(chunks)]
                 + [(epi_gate, (sga_ref, c), rows_dot, (SEC_GA, s)) for c, s in enumerate(chunks)]
                 + [(epi_gate, (sgb_ref, c), rows_dot, (SEC_GB, s)) for c, s in enumerate(chunks)])
        light = ([(epi_k, (c,), rows_dot, (SEC_K, s)) for c, s in enumerate(chunks)]
                 + [(epi_q, (c,), feat_dot, (wqT_ref, s)) for c, s in enumerate(chunks)]
                 + [(epi_v, (c,), feat_dot, (wvT_ref, s)) for c, s in enumerate(chunks)])
        order = []
        while heavy or light:
            order += heavy[:1] + light[:1]
            heavy, light = heavy[1:], light[1:]
        tri = (lax.broadcasted_iota(jnp.int32, (CHUNK, CHUNK), 0)
               >= lax.broadcasted_iota(jnp.int32, (CHUNK, CHUNK), 1))

        def spatial(gi):
            ws = jnp.where(tri, ws_ref[gi], 0.0).astype(BF16)
            c, local = group_cols(gi)
            vcat = jnp.concatenate([v_parts[c][rows, local] for rows in chunk_rows], axis=1)
            sv = jnp.dot(ws, vcat, preferred_element_type=F32) + bs_ref[:, gi:gi + 1]
            put(epi_sv, gi, sv)

        early, order = order[:INPROJ_EARLY_CHUNKS], order[INPROJ_EARLY_CHUNKS:]
        assert all(dot_fn is rows_dot for _, _, dot_fn, _ in early)
        h_slabs, z_slabs = [], [[] for _ in early]
        for lo in range(0, R, INPROJ_SLAB):
            h_slabs.append(normed(slice(lo, lo + INPROJ_SLAB)))
            for zs, (_, _, _, dot_args) in zip(z_slabs, early):
                zs.append(rows_dot(*dot_args, lhs=h_slabs[-1]))
        h = jnp.concatenate(h_slabs, axis=0)
        for zs, (epi, epi_args, _, _) in zip(z_slabs, early):
            put(epi, *epi_args, jnp.concatenate(zs, axis=0))
        yield True

        spatial_todo = list(range(ws_ref.shape[0]))
        for epi, epi_args, dot_fn, dot_args in order:
            put(epi, *epi_args, dot_fn(*dot_args))
            yield True
            if spatial_todo and len(v_parts) == len(chunks) and len(u_parts) == len(chunks):
                spatial(spatial_todo.pop(0))
                yield True
        for gi in spatial_todo:
            spatial(gi)
            yield True

    live = [row_group(j) for j in range(tm // R)]
    while live or queue:
        live = [g for g in live if next(g, None) is not None]
        rnd[0] += 1
        ready = sum(1 for r, _ in queue if r < rnd[0] - 1) if live else len(queue)
        for _ in range(min(ready, 2 * len(live) if live else ready)):
            queue.pop(0)[1]()


def _inproj(x, mod3, norm1_g, w_in, ln_v_g, ln_v_b, w_spatial, b_spatial):
    B, S, D = x.shape
    T = SEQ_TILE
    nt = S // T
    tm = INPROJ_TILE
    w_bf = w_in.astype(BF16)
    wqT = w_bf[:, SEC_Q * D:(SEC_Q + 1) * D].T
    wvT = w_bf[:, SEC_V * D:(SEC_V + 1) * D].T
    G = w_spatial.shape[0]
    row = lambda a: a.reshape(1, D)
    tok = pl.BlockSpec((None, tm, D), lambda b, t: (b, t, 0))
    q_featT = pl.BlockSpec((None, tm // T, D, T), lambda b, t: (b, t, 0, 0))
    v_featT = pl.BlockSpec((None, D, tm), lambda b, t: (b, 0, t))
    out_tok = jax.ShapeDtypeStruct((B, S, D), BF16)
    return pl.pallas_call(
        _inproj_kernel,
        grid=(B, S // tm),
        in_specs=[tok,
                  pl.BlockSpec((None, 6, D), lambda b, t: (b, 0, 0)),
                  _resident((1, D)),
                  _resident((D, D)), _resident((D, D)), _resident(w_bf.shape),
                  _resident((1, D)), _resident((1, D)),
                  _resident((G, CHUNK, CHUNK)), _resident((CHUNK, G))],
        out_specs=[q_featT, tok, v_featT, tok, tok, tok],
        out_shape=[jax.ShapeDtypeStruct((B, nt, D, T), BF16), out_tok,
                   jax.ShapeDtypeStruct((B, D, S), BF16), out_tok, out_tok, out_tok],
        compiler_params=pltpu.CompilerParams(
            dimension_semantics=("arbitrary", "arbitrary"),
            vmem_limit_bytes=_vmem_limit(resident=(2 * D * D + w_bf.size) * 2,
                                         streamed=tm * D * 4 + 6 * tm * D * 2,
                                         temporaries=8 * tm * D * 4)),
        name="inproj",
    )(x, mod3, row(norm1_g), wqT, wvT, w_bf, row(ln_v_g), row(ln_v_b),
      w_spatial, b_spatial.T)


def _attn_kernel(qT_ref, k_ref, vT_ref, tbl_ref, lam_ref, sg_ref, o_ref, va_ref):
    nt, _, T = qT_ref.shape
    n_heads = va_ref.shape[0]
    feat = lax.broadcasted_iota(jnp.int32, (V_DIM, T), 0)
    head_cols = [slice(hl * V_DIM, (hl + 1) * V_DIM) for hl in range(n_heads)]
    for hl in range(n_heads):
        va_ref[hl, 0:V_DIM, :] = vT_ref[head_cols[hl], :]
        va_ref[hl, V_DIM:, :] = jnp.ones((ONES_ROWS, va_ref.shape[2]), BF16)
    lam = (jnp.exp(jnp.sum(lam_ref[0:1, :] * lam_ref[1:2, :], axis=-1, keepdims=True))
           - jnp.exp(jnp.sum(lam_ref[2:3, :] * lam_ref[3:4, :], axis=-1, keepdims=True))
           + LAM_INIT)

    units = [(hl, i, mi) for i in reversed(range(nt)) for hl in range(n_heads) for mi in range(2)]
    tick = [0]
    s_done = {}
    p_done = {}
    outs = {}

    def key_chunks(i):
        near_lo, hi = max(i - 1, 0) * T, (i + 1) * T
        cuts = sorted(set(range(near_lo, -1, -KEY_CHUNK)) | {0})
        far = [(lo, nxt, None) for lo, nxt in zip(cuts[:-1], cuts[1:])]
        return far + [(near_lo, hi, slice(2 * T - (hi - near_lo), 2 * T))]

    def scores():
        for u in units:
            hl, i, mi = u
            in_map = (feat < HEAD_DIM) if mi == 0 else (feat >= HEAD_DIM)
            qz = jnp.where(in_map, qT_ref[i, head_cols[hl], :].astype(F32), 0.0).astype(BF16)
            for c, (lo, hi, tbl_rows) in enumerate(key_chunks(i)):
                s = jnp.dot(k_ref[lo:hi, head_cols[hl]], qz, preferred_element_type=F32)
                if tbl_rows is not None:
                    s = s + tbl_ref[hl, tbl_rows, :]
                s_done[u, c] = (s, tick[0])
                yield True

    def probs():
        for u in units:
            for c in range(len(key_chunks(u[1]))):
                while (u, c) not in s_done or tick[0] < s_done[u, c][1] + EXP_LAG:
                    yield False
                s, _ = s_done.pop((u, c))
                mc = jnp.max(s, axis=0, keepdims=True)
                p_done[u, c] = (jnp.exp2((s - mc).astype(BF16)), mc, tick[0])
                yield True

    def values():
        for u in units:
            hl, i, mi = u
            parts = []
            for c, (lo, hi, _) in enumerate(key_chunks(i)):
                while (u, c) not in p_done or tick[0] < p_done[u, c][2] + VALUE_LAG:
                    yield False
                p, mc, _ = p_done.pop((u, c))
                parts.append((jnp.dot(va_ref[hl, :, lo:hi], p, preferred_element_type=F32), mc))
                yield True
            m = functools.reduce(jnp.maximum, [mc for _, mc in parts])
            acc = functools.reduce(lambda a, b: a + b, [d * jnp.exp2(mc - m) for d, mc in parts])
            outs[u] = acc[0:V_DIM, :] * (1.0 / acc[V_DIM:V_DIM + 1, :])
            if mi == 1:
                oT = outs.pop((hl, i, 0)) - lam * outs.pop((hl, i, 1))
                yT = oT * lax.rsqrt(jnp.mean(oT * oT, axis=0, keepdims=True) + RMS_EPS)
                y = yT.T * sg_ref[...] * (1.0 - LAM_INIT)
                o_ref[i * T:(i + 1) * T, head_cols[hl]] = y.astype(BF16)

    streams = [scores(), probs(), values()]
    while streams:
        streams = [g for g in streams if next(g, None) is not None]
        tick[0] += 1


def _attention(qT, k, vT, tbl, lam_vecs, subln_g):
    B, nt, D, T = qT.shape
    S = nt * T
    H = D // V_DIM
    hp = ATTN_HEADS_PER_STEP
    W = hp * V_DIM
    return pl.pallas_call(
        _attn_kernel,
        grid=(B, H // hp),
        in_specs=[pl.BlockSpec((None, nt, W, T), lambda b, h: (b, 0, h, 0)),
                  pl.BlockSpec((None, S, W), lambda b, h: (b, 0, h)),
                  pl.BlockSpec((None, W, S), lambda b, h: (b, h, 0)),
                  pl.BlockSpec((hp, 2 * T, T), lambda b, h: (h, 0, 0)),
                  pl.BlockSpec((4, HEAD_DIM), lambda b, h: (0, 0)),
                  pl.BlockSpec((1, V_DIM), lambda b, h: (0, 0))],
        out_specs=pl.BlockSpec((None, S, W), lambda b, h: (b, 0, h)),
        out_shape=jax.ShapeDtypeStruct((B, S, D), BF16),
        scratch_shapes=[pltpu.VMEM((hp, V_DIM + ONES_ROWS, S), BF16)],
        compiler_params=pltpu.CompilerParams(
            dimension_semantics=("arbitrary", "arbitrary"),
            vmem_limit_bytes=_vmem_limit(resident=hp * (V_DIM + ONES_ROWS) * S * 2,
                                         streamed=4 * S * W * 2 + hp * 2 * T * T * 4,
                                         temporaries=ATTN_TEMP_BYTES)),
        name="attn",
    )(qT, k, vT, tbl, lam_vecs, subln_g.reshape(1, V_DIM))


def _tail_kernel(x_ref, oa_ref, ob_ref, sga_ref, sgb_ref, mod_ref, n2g_ref, fg_ref,
                 wpa_ref, wpb_ref, wo_ref, wfi_ref, wfo_ref, o_ref):
    hidden = wfo_ref.shape[0]
    tm = x_ref.shape[0]
    bounds = list(range(0, hidden, TAIL_HIDDEN_CHUNK)) + [hidden]

    def row_group(rows):
        pa = jnp.dot(oa_ref[rows, :], wpa_ref[...], preferred_element_type=F32)
        pb = jnp.dot(ob_ref[rows, :], wpb_ref[...], preferred_element_type=F32)
        yield True
        merged = (sga_ref[rows, :].astype(F32) * pa + sgb_ref[rows, :].astype(F32) * pb)
        x1 = x_ref[rows, :] + mod_ref[2:3, :] * jnp.dot(merged.astype(BF16), wo_ref[...],
                                                        preferred_element_type=F32)
        yield True
        y = x1 * lax.rsqrt(jnp.mean(x1 * x1, axis=-1, keepdims=True) + RMS_EPS) * n2g_ref[...]
        h2 = (y * (1.0 + mod_ref[4:5, :]) + mod_ref[3:4, :]).astype(BF16)
        ffn = None
        for lo, hi in zip(bounds[:-1], bounds[1:]):
            gate = jnp.dot(h2, wfi_ref[:, lo:hi], preferred_element_type=F32)
            up = jnp.dot(h2, wfi_ref[:, hidden + lo:hidden + hi], preferred_element_type=F32)
            yield True
            act = (gate * _sigmoid(gate) * up).astype(BF16)
            part = jnp.dot(act, wfo_ref[lo:hi, :], preferred_element_type=F32)
            ffn = part if ffn is None else ffn + part
        yield True
        x2 = x1 + mod_ref[5:6, :] * ffn
        o_ref[rows, :] = (x2 * lax.rsqrt(jnp.mean(x2 * x2, axis=-1, keepdims=True) + RMS_EPS)
                          * fg_ref[...])

    live = [row_group(slice(r, r + TAIL_ROWS)) for r in range(0, tm, TAIL_ROWS)]
    while live:
        live = [g for g in live if next(g, None) is not None]


def _tail(x, oa, ob, sga, sgb, mod3, norm2_g, final_g, w_proj_a, w_proj_b, w_out, w_ffn_in,
          w_ffn_out):
    B, S, D = x.shape
    T = TAIL_TILE
    hidden = w_ffn_out.shape[0]
    row = lambda a: a.reshape(1, D)
    tok = pl.BlockSpec((None, T, D), lambda b, t: (b, t, 0))
    return pl.pallas_call(
        _tail_kernel,
        grid=(B, S // T),
        in_specs=[tok, tok, tok, tok, tok,
                  pl.BlockSpec((None, 6, D), lambda b, t: (b, 0, 0)),
                  _resident((1, D)), _resident((1, D)),
                  _resident((D, D)), _resident((D, D)), _resident((D, D)),
                  _resident((D, 2 * hidden)), _resident((hidden, D))],
        out_specs=tok,
        out_shape=jax.ShapeDtypeStruct((B, S, D), F32),
        compiler_params=pltpu.CompilerParams(
            dimension_semantics=("arbitrary", "arbitrary"),
            vmem_limit_bytes=_vmem_limit(
                resident=(3 * D * D + 3 * D * hidden) * 2,
                streamed=2 * T * D * 4 + 4 * T * D * 2,
                temporaries=(T // TAIL_ROWS) * TAIL_ROWS * (6 * D + 2 * TAIL_HIDDEN_CHUNK) * 4)),
        name="tail",
    )(x, oa, ob, sga, sgb, mod3, row(norm2_g), row(final_g),
      w_proj_a.astype(BF16), w_proj_b.astype(BF16), w_out.astype(BF16),
      w_ffn_in.astype(BF16), w_ffn_out.astype(BF16))


def kernel(x, c, w_ada, b_ada, norm1_g, norm2_g, w_in, lambda_q1, lambda_k1, lambda_q2, lambda_k2,
           subln_g, ln_v_g, ln_v_b, w_spatial, b_spatial, w_proj_a, w_proj_b, w_out, w_ffn_in,
           w_ffn_out, rel_bias, final_g):
    B, S, D = x.shape
    assert w_ada.shape[0] == 1, "single-layer block"
    assert S % INPROJ_TILE == 0 and S % TAIL_TILE == 0
    assert INPROJ_TILE % INPROJ_ROWS == 0 and INPROJ_ROWS % SEQ_TILE == 0
    assert INPROJ_ROWS % CHUNK == 0 and TAIL_TILE % TAIL_ROWS == 0
    assert SEQ_TILE >= REL_MAX_DIST, "bias must be constant beyond the first off-diagonal tile"

    mod3 = _ada(c, w_ada[0], b_ada[0]).reshape(B, 6, D)
    tbl = _bias_tiles(rel_bias, SEQ_TILE)
    qT, k, vT, ob, sga, sgb = _inproj(x, mod3, norm1_g[0], w_in[0], ln_v_g[0], ln_v_b[0],
                                      w_spatial[0], b_spatial[0])
    lam_vecs = jnp.concatenate([lambda_q1, lambda_k1, lambda_q2, lambda_k2], axis=0)
    oa = _attention(qT, k, vT, tbl, lam_vecs, subln_g[0])
    return _tail(x, oa, ob, sga, sgb, mod3, norm2_g[0], final_g, w_proj_a[0], w_proj_b[0],
                 w_out[0], w_ffn_in[0], w_ffn_out[0])
```

```python
import functools
import math

import numpy as np
import jax
import jax.numpy as jnp
from jax import lax
from jax.experimental import pallas as pl
from jax.experimental.pallas import tpu as pltpu

F32 = jnp.float32
BF16 = jnp.bfloat16

RMS_EPS = 1e-6
LN_EPS = 1e-5
HEAD_DIM = 64
V_DIM = 2 * HEAD_DIM
CHUNK = 128
GROUP_WIDTH = 128
REL_BUCKETS = 32
REL_MAX_EXACT = REL_BUCKETS // 2
REL_MAX_DIST = 128
LAM_INIT = 0.8 - 0.6 * math.exp(-0.3 * 0)
MASK_VALUE = -1e30
LOG2E = math.log2(math.e)
ONES_ROWS = 16
ATTN_HEADS_PER_STEP = 2
KEY_CHUNK = 256
EXP_LAG = 8
VALUE_LAG = 2

SEQ_TILE = 256
INPROJ_TILE = 512
INPROJ_ROWS = 512
INPROJ_SLAB = 128
INPROJ_EARLY_CHUNKS = 2
INPROJ_COLS = 256
TAIL_TILE = 512
TAIL_ROWS = 256
TAIL_HIDDEN_CHUNK = 1536
V7X_VMEM_BYTES = 64 * 1024 * 1024
VMEM_HEADROOM = 6 * 1024 * 1024
ATTN_TEMP_BYTES = 46 * 1024 * 1024

SEC_Q, SEC_K, SEC_V, SEC_U, SEC_G, SEC_GA, SEC_GB = range(7)


def _sigmoid(x):
    return 1.0 / (1.0 + jnp.exp2(x * (-LOG2E)))


def _gelu_tanh(x):
    a = -2.0 * math.sqrt(2.0 / math.pi) * LOG2E
    return x / (1.0 + jnp.exp2(x * (a + (a * 0.044715) * (x * x))))


def _vmem_limit(resident, streamed, temporaries):
    return int(min(resident + 2 * streamed + temporaries, V7X_VMEM_BYTES - VMEM_HEADROOM))


def _resident(shape):
    n = len(shape)
    return pl.BlockSpec(shape, lambda *_: (0,) * n, pipeline_mode=pl.Buffered(1))


def _ada_kernel(c_ref, w_ref, b_ref, o_ref):
    c = c_ref[...]
    ca = (c * _sigmoid(c)).astype(BF16)
    o_ref[...] = jnp.dot(ca, w_ref[...].astype(BF16), preferred_element_type=F32) + b_ref[...]


def _ada(c, w_ada, b_ada):
    B, D = c.shape
    N = w_ada.shape[1]
    tn = 1024
    return pl.pallas_call(
        _ada_kernel,
        grid=(N // tn,),
        in_specs=[pl.BlockSpec((B, D), lambda n: (0, 0)),
                  pl.BlockSpec((D, tn), lambda n: (0, n)),
                  pl.BlockSpec((1, tn), lambda n: (0, n))],
        out_specs=pl.BlockSpec((B, tn), lambda n: (0, n)),
        out_shape=jax.ShapeDtypeStruct((B, N), F32),
        name="ada",
    )(c, w_ada, b_ada.reshape(1, N))


def _bucket_tiles(T):
    kk = np.arange(T, dtype=np.int32)[:, None]
    qq = np.arange(T, dtype=np.int32)[None, :]
    out = []
    for o in (1, 0):
        dist = o * T + qq - kk
        n = np.maximum(dist, 0)
        nf = np.maximum(n, 1).astype(np.float32)
        large = REL_MAX_EXACT + (np.log(nf / np.float32(REL_MAX_EXACT))
                                 / np.float32(math.log(REL_MAX_DIST / REL_MAX_EXACT))
                                 * np.float32(REL_BUCKETS - REL_MAX_EXACT)).astype(np.int32)
        large = np.minimum(large, REL_BUCKETS - 1)
        b = np.where(n < REL_MAX_EXACT, n, large)
        out.append(np.where(dist < 0, -1, b).astype(np.int32))
    return np.concatenate(out, axis=0)


def _bias_kernel(rb_ref, bucket_ref, o_ref, *, layout):
    h = pl.program_id(0)
    R = bucket_ref.shape[-1]
    far = rb_ref[REL_BUCKETS - 1, h]
    blocks = []
    for n in range(bucket_ref.shape[0]):
        bk = bucket_ref[n]
        acc = jnp.where(bk < 0, MASK_VALUE, 0.0).astype(F32)
        for b in range(REL_BUCKETS - 1):
            acc = jnp.where(bk == b, (rb_ref[b, h] - far) * LOG2E, acc)
        blocks.append(acc)
    for r, c, kind in layout:
        if kind == "far":
            blk = jnp.zeros((R, R), F32)
        elif kind == "masked":
            blk = jnp.full((R, R), MASK_VALUE, F32)
        else:
            blk = blocks[kind]
        o_ref[r:r + R, c:c + R] = blk


def _bias_tiles(rel_bias, T):
    H = rel_bias.shape[1]
    R = REL_MAX_DIST
    assert T % R == 0
    full = _bucket_tiles(T)
    distinct, index, layout = [], {}, []
    for r in range(0, 2 * T, R):
        for c in range(0, T, R):
            blk = full[r:r + R, c:c + R]
            if (blk == REL_BUCKETS - 1).all():
                kind = "far"
            elif (blk == -1).all():
                kind = "masked"
            else:
                kind = index.setdefault(blk.tobytes(), len(distinct))
                if kind == len(distinct):
                    distinct.append(blk)
            layout.append((r, c, kind))
    buckets = jnp.asarray(np.stack(distinct))
    return pl.pallas_call(
        functools.partial(_bias_kernel, layout=layout),
        grid=(H,),
        in_specs=[pl.BlockSpec(memory_space=pltpu.SMEM),
                  pl.BlockSpec(buckets.shape, lambda h: (0, 0, 0))],
        out_specs=pl.BlockSpec((None, 2 * T, T), lambda h: (h, 0, 0)),
        out_shape=jax.ShapeDtypeStruct((H, 2 * T, T), F32),
        name="bias_tiles",
    )(rel_bias, buckets)


def _inproj_kernel(x_ref, mod_ref, n1g_ref, w_ref, lng_ref, lnb_ref, ws_ref, bs_ref,
                   qT_ref, k_ref, vT_ref, ob_ref, sga_ref, sgb_ref):
    tm, D = x_ref.shape
    NC = INPROJ_COLS
    T = qT_ref.shape[-1]
    chunks = [slice(c * NC, (c + 1) * NC) for c in range(D // NC)]

    rnd = [0]
    queue = []

    def put(fn, *args):
        queue.append((rnd[0], functools.partial(fn, *args)))

    R = INPROJ_ROWS

    def row_group(j):
        rs = slice(j * R, (j + 1) * R)
        g_parts, u_parts, v_parts, stats = {}, {}, {}, {}

        def normed(rows):
            x = x_ref[rs.start + rows.start:rs.start + rows.stop, :]
            y = x * lax.rsqrt(jnp.mean(x * x, axis=-1, keepdims=True) + RMS_EPS) * n1g_ref[...]
            return (y * (1.0 + mod_ref[1:2, :]) + mod_ref[0:1, :]).astype(BF16)

        def epi_g(c, z):
            g = _gelu_tanh(z)
            g_parts[c] = g
            rsum = jnp.sum(g, axis=-1, keepdims=True)
            stats["sum"] = rsum if c == 0 else stats["sum"] + rsum
            if c == len(chunks) - 1:
                for cc in range(len(chunks)):
                    put(ln_center, cc)
                for cc in range(len(chunks)):
                    put(ln_scale, cc)

        def ln_center(c):
            gc = g_parts[c] - stats["sum"] * (1.0 / D)
            g_parts[c] = gc
            sq = jnp.sum(gc * gc, axis=-1, keepdims=True)
            stats["sq"] = sq if c == 0 else stats["sq"] + sq

        def ln_scale(c):
            rstd = lax.rsqrt(stats["sq"] * (1.0 / D) + LN_EPS)
            v = g_parts.pop(c) * rstd * lng_ref[:, chunks[c]] + lnb_ref[:, chunks[c]]
            v_parts[c] = v.astype(BF16)

        def epi_u(c, z):
            u_parts[c] = _gelu_tanh(z)

        def epi_k(c, z):
            k_ref[rs, chunks[c]] = z.astype(BF16)

        def epi_q(c, z):
            qz = (z.T * (LOG2E / math.sqrt(HEAD_DIM))).astype(BF16)
            for jj in range(R // T):
                qT_ref[j * (R // T) + jj, chunks[c], :] = qz[:, jj * T:(jj + 1) * T]

        def epi_v(c, z):
            vT_ref[chunks[c], rs] = z.T.astype(BF16)

        def epi_gate(ref, c, z):
            ref[rs, chunks[c]] = _sigmoid(z).astype(BF16)

        chunk_rows = [slice(ci * CHUNK, (ci + 1) * CHUNK) for ci in range(R // CHUNK)]

        def group_cols(gi):
            lo = gi * GROUP_WIDTH
            return lo // NC, slice(lo % NC, lo % NC + GROUP_WIDTH)

        def epi_sv(gi, sv):
            c, local = group_cols(gi)
            cols = slice(gi * GROUP_WIDTH, (gi + 1) * GROUP_WIDTH)
            for ci, rows in enumerate(chunk_rows):
                sv_c = sv[:, ci * GROUP_WIDTH:(ci + 1) * GROUP_WIDTH]
                out_rows = slice(j * R + rows.start, j * R + rows.stop)
                ob_ref[out_rows, cols] = (u_parts[c][rows, local] * sv_c).astype(BF16)

        def rows_dot(sec, cols, lhs=None):
            return jnp.dot(h if lhs is None else lhs,
                           w_ref[:, sec * D + cols.start:sec * D + cols.stop],
                           preferred_element_type=F32)

        heavy = ([(epi_g, (c,), rows_dot, (SEC_G, s)) for c, s in enumerate(chunks)]
                 + [(epi_u, (c,), rows_dot, (SEC_U, s)) for c, s in enumerate(chunks)]
                 + [(epi_gate, (sga_ref, c), rows_dot, (SEC_GA, s)) for c, s in enumerate(chunks)]
                 + [(epi_gate, (sgb_ref, c), rows_dot, (SEC_GB, s)) for c, s in enumerate(chunks)])
        light = ([(epi_k, (c,), rows_dot, (SEC_K, s)) for c, s in enumerate(chunks)]
                 + [(epi_q, (c,), rows_dot, (SEC_Q, s)) for c, s in enumerate(chunks)]
                 + [(epi_v, (c,), rows_dot, (SEC_V, s)) for c, s in enumerate(chunks)])
        order = []
        while heavy or light:
            order += heavy[:1] + light[:1]
            heavy, light = heavy[1:], light[1:]
        tri = (lax.broadcasted_iota(jnp.int32, (CHUNK, CHUNK), 0)
               >= lax.broadcasted_iota(jnp.int32, (CHUNK, CHUNK), 1))

        def spatial(gi):
            ws = jnp.where(tri, ws_ref[gi], 0.0).astype(BF16)
            c, local = group_cols(gi)
            vcat = jnp.concatenate([v_parts[c][rows, local] for rows in chunk_rows], axis=1)
            sv = jnp.dot(ws, vcat, preferred_element_type=F32) + bs_ref[:, gi:gi + 1]
            put(epi_sv, gi, sv)

        early, order = order[:INPROJ_EARLY_CHUNKS], order[INPROJ_EARLY_CHUNKS:]
        assert all(dot_fn is rows_dot for _, _, dot_fn, _ in early)
        h_slabs, z_slabs = [], [[] for _ in early]
        for lo in range(0, R, INPROJ_SLAB):
            h_slabs.append(normed(slice(lo, lo + INPROJ_SLAB)))
            for zs, (_, _, _, dot_args) in zip(z_slabs, early):
                zs.append(rows_dot(*dot_args, lhs=h_slabs[-1]))
        h = jnp.concatenate(h_slabs, axis=0)
        for zs, (epi, epi_args, _, _) in zip(z_slabs, early):
            put(epi, *epi_args, jnp.concatenate(zs, axis=0))
        yield True

        spatial_todo = list(range(ws_ref.shape[0]))
        for epi, epi_args, dot_fn, dot_args in order:
            put(epi, *epi_args, dot_fn(*dot_args))
            yield True
            if spatial_todo and len(v_parts) == len(chunks) and len(u_parts) == len(chunks):
                spatial(spatial_todo.pop(0))
                yield True
        for gi in spatial_todo:
            spatial(gi)
            yield True

    live = [row_group(j) for j in range(tm // R)]
    while live or queue:
        live = [g for g in live if next(g, None) is not None]
        rnd[0] += 1
        ready = sum(1 for r, _ in queue if r < rnd[0] - 1) if live else len(queue)
        for _ in range(min(ready, 2 * len(live) if live else ready)):
            queue.pop(0)[1]()


def _inproj(x, mod3, norm1_g, w_in, ln_v_g, ln_v_b, w_spatial, b_spatial):
    B, S, D = x.shape
    T = SEQ_TILE
    nt = S // T
    tm = INPROJ_TILE
    w_bf = w_in.astype(BF16)
    G = w_spatial.shape[0]
    row = lambda a: a.reshape(1, D)
    tok = pl.BlockSpec((None, tm, D), lambda b, t: (b, t, 0))
    q_featT = pl.BlockSpec((None, tm // T, D, T), lambda b, t: (b, t, 0, 0))
    v_featT = pl.BlockSpec((None, D, tm), lambda b, t: (b, 0, t))
    out_tok = jax.ShapeDtypeStruct((B, S, D), BF16)
    return pl.pallas_call(
        _inproj_kernel,
        grid=(B, S // tm),
        in_specs=[tok,
                  pl.BlockSpec((None, 6, D), lambda b, t: (b, 0, 0)),
                  _resident((1, D)),
                  _resident(w_bf.shape),
                  _resident((1, D)), _resident((1, D)),
                  _resident((G, CHUNK, CHUNK)), _resident((CHUNK, G))],
        out_specs=[q_featT, tok, v_featT, tok, tok, tok],
        out_shape=[jax.ShapeDtypeStruct((B, nt, D, T), BF16), out_tok,
                   jax.ShapeDtypeStruct((B, D, S), BF16), out_tok, out_tok, out_tok],
        compiler_params=pltpu.CompilerParams(
            dimension_semantics=("arbitrary", "arbitrary"),
            vmem_limit_bytes=_vmem_limit(resident=w_bf.size * 2,
                                         streamed=tm * D * 4 + 6 * tm * D * 2,
                                         temporaries=8 * tm * D * 4)),
        name="inproj",
    )(x, mod3, row(norm1_g), w_bf, row(ln_v_g), row(ln_v_b),
      w_spatial, b_spatial.T)


def _attn_kernel(qT_ref, k_ref, vT_ref, tbl_ref, lam_ref, sg_ref, o_ref, va_ref):
    nt, _, T = qT_ref.shape
    n_heads = va_ref.shape[0]
    feat = lax.broadcasted_iota(jnp.int32, (V_DIM, T), 0)
    head_cols = [slice(hl * V_DIM, (hl + 1) * V_DIM) for hl in range(n_heads)]
    for hl in range(n_heads):
        va_ref[hl, 0:V_DIM, :] = vT_ref[head_cols[hl], :]
        va_ref[hl, V_DIM:, :] = jnp.ones((ONES_ROWS, va_ref.shape[2]), BF16)
    lam = (jnp.exp(jnp.sum(lam_ref[0:1, :] * lam_ref[1:2, :], axis=-1, keepdims=True))
           - jnp.exp(jnp.sum(lam_ref[2:3, :] * lam_ref[3:4, :], axis=-1, keepdims=True))
           + LAM_INIT)

    units = [(hl, i, mi) for i in reversed(range(nt)) for hl in range(n_heads) for mi in range(2)]
    tick = [0]
    s_done = {}
    p_done = {}
    outs = {}

    def key_chunks(i):
        near_lo, hi = max(i - 1, 0) * T, (i + 1) * T
        cuts = sorted(set(range(near_lo, -1, -KEY_CHUNK)) | {0})
        far = [(lo, nxt, None) for lo, nxt in zip(cuts[:-1], cuts[1:])]
        return far + [(near_lo, hi, slice(2 * T - (hi - near_lo), 2 * T))]

    def scores():
        for u in units:
            hl, i, mi = u
            in_map = (feat < HEAD_DIM) if mi == 0 else (feat >= HEAD_DIM)
            qz = jnp.where(in_map, qT_ref[i, head_cols[hl], :].astype(F32), 0.0).astype(BF16)
            for c, (lo, hi, tbl_rows) in enumerate(key_chunks(i)):
                s = jnp.dot(k_ref[lo:hi, head_cols[hl]], qz, preferred_element_type=F32)
                if tbl_rows is not None:
                    s = s + tbl_ref[hl, tbl_rows, :]
                s_done[u, c] = (s, tick[0])
                yield True

    def probs():
        for u in units:
            for c in range(len(key_chunks(u[1]))):
                while (u, c) not in s_done or tick[0] < s_done[u, c][1] + EXP_LAG:
                    yield False
                s, _ = s_done.pop((u, c))
                mc = jnp.max(s, axis=0, keepdims=True)
                p_done[u, c] = (jnp.exp2((s - mc).astype(BF16)), mc, tick[0])
                yield True

    def values():
        for u in units:
            hl, i, mi = u
            parts = []
            for c, (lo, hi, _) in enumerate(key_chunks(i)):
                while (u, c) not in p_done or tick[0] < p_done[u, c][2] + VALUE_LAG:
                    yield False
                p, mc, _ = p_done.pop((u, c))
                parts.append((jnp.dot(va_ref[hl, :, lo:hi], p, preferred_element_type=F32), mc))
                yield True
            m = functools.reduce(jnp.maximum, [mc for _, mc in parts])
            acc = functools.reduce(lambda a, b: a + b, [d * jnp.exp2(mc - m) for d, mc in parts])
            outs[u] = acc[0:V_DIM, :] * (1.0 / acc[V_DIM:V_DIM + 1, :])
            if mi == 1:
                oT = outs.pop((hl, i, 0)) - lam * outs.pop((hl, i, 1))
                yT = oT * lax.rsqrt(jnp.mean(oT * oT, axis=0, keepdims=True) + RMS_EPS)
                y = yT.T * sg_ref[...] * (1.0 - LAM_INIT)
                o_ref[i * T:(i + 1) * T, head_cols[hl]] = y.astype(BF16)

    streams = [scores(), probs(), values()]
    while streams:
        streams = [g for g in streams if next(g, None) is not None]
        tick[0] += 1


def _attention(qT, k, vT, tbl, lam_vecs, subln_g):
    B, nt, D, T = qT.shape
    S = nt * T
    H = D // V_DIM
    hp = ATTN_HEADS_PER_STEP
    W = hp * V_DIM
    return pl.pallas_call(
        _attn_kernel,
        grid=(B, H // hp),
        in_specs=[pl.BlockSpec((None, nt, W, T), lambda b, h: (b, 0, h, 0)),
                  pl.BlockSpec((None, S, W), lambda b, h: (b, 0, h)),
                  pl.BlockSpec((None, W, S), lambda b, h: (b, h, 0)),
                  pl.BlockSpec((hp, 2 * T, T), lambda b, h: (h, 0, 0)),
                  pl.BlockSpec((4, HEAD_DIM), lambda b, h: (0, 0)),
                  pl.BlockSpec((1, V_DIM), lambda b, h: (0, 0))],
        out_specs=pl.BlockSpec((None, S, W), lambda b, h: (b, 0, h)),
        out_shape=jax.ShapeDtypeStruct((B, S, D), BF16),
        scratch_shapes=[pltpu.VMEM((hp, V_DIM + ONES_ROWS, S), BF16)],
        compiler_params=pltpu.CompilerParams(
            dimension_semantics=("arbitrary", "arbitrary"),
            vmem_limit_bytes=_vmem_limit(resident=hp * (V_DIM + ONES_ROWS) * S * 2,
                                         streamed=4 * S * W * 2 + hp * 2 * T * T * 4,
                                         temporaries=ATTN_TEMP_BYTES)),
        name="attn",
    )(qT, k, vT, tbl, lam_vecs, subln_g.reshape(1, V_DIM))


def _tail_kernel(x_ref, oa_ref, ob_ref, sga_ref, sgb_ref, mod_ref, n2g_ref, fg_ref,
                 wpa_ref, wpb_ref, wo_ref, wfi_ref, wfo_ref, o_ref):
    hidden = wfo_ref.shape[0]
    tm = x_ref.shape[0]
    bounds = list(range(0, hidden, TAIL_HIDDEN_CHUNK)) + [hidden]

    def row_group(rows):
        pa = jnp.dot(oa_ref[rows, :], wpa_ref[...], preferred_element_type=F32)
        pb = jnp.dot(ob_ref[rows, :], wpb_ref[...], preferred_element_type=F32)
        yield True
        merged = (sga_ref[rows, :].astype(F32) * pa + sgb_ref[rows, :].astype(F32) * pb)
        x1 = x_ref[rows, :] + mod_ref[2:3, :] * jnp.dot(merged.astype(BF16), wo_ref[...],
                                                        preferred_element_type=F32)
        yield True
        y = x1 * lax.rsqrt(jnp.mean(x1 * x1, axis=-1, keepdims=True) + RMS_EPS) * n2g_ref[...]
        h2 = (y * (1.0 + mod_ref[4:5, :]) + mod_ref[3:4, :]).astype(BF16)
        ffn = None
        for lo, hi in zip(bounds[:-1], bounds[1:]):
            gate = jnp.dot(h2, wfi_ref[:, lo:hi], preferred_element_type=F32)
            up = jnp.dot(h2, wfi_ref[:, hidden + lo:hidden + hi], preferred_element_type=F32)
            yield True
            act = (gate * _sigmoid(gate) * up).astype(BF16)
            part = jnp.dot(act, wfo_ref[lo:hi, :], preferred_element_type=F32)
            ffn = part if ffn is None else ffn + part
        yield True
        x2 = x1 + mod_ref[5:6, :] * ffn
        o_ref[rows, :] = (x2 * lax.rsqrt(jnp.mean(x2 * x2, axis=-1, keepdims=True) + RMS_EPS)
                          * fg_ref[...])

    live = [row_group(slice(r, r + TAIL_ROWS)) for r in range(0, tm, TAIL_ROWS)]
    while live:
        live = [g for g in live if next(g, None) is not None]


def _tail(x, oa, ob, sga, sgb, mod3, norm2_g, final_g, w_proj_a, w_proj_b, w_out, w_ffn_in,
          w_ffn_out):
    B, S, D = x.shape
    T = TAIL_TILE
    hidden = w_ffn_out.shape[0]
    row = lambda a: a.reshape(1, D)
    tok = pl.BlockSpec((None, T, D), lambda b, t: (b, t, 0))
    return pl.pallas_call(
        _tail_kernel,
        grid=(B, S // T),
        in_specs=[tok, tok, tok, tok, tok,
                  pl.BlockSpec((None, 6, D), lambda b, t: (b, 0, 0)),
                  _resident((1, D)), _resident((1, D)),
                  _resident((D, D)), _resident((D, D)), _resident((D, D)),
                  _resident((D, 2 * hidden)), _resident((hidden, D))],
        out_specs=tok,
        out_shape=jax.ShapeDtypeStruct((B, S, D), F32),
        compiler_params=pltpu.CompilerParams(
            dimension_semantics=("arbitrary", "arbitrary"),
            vmem_limit_bytes=_vmem_limit(
                resident=(3 * D * D + 3 * D * hidden) * 2,
                streamed=2 * T * D * 4 + 4 * T * D * 2,
                temporaries=(T // TAIL_ROWS) * TAIL_ROWS * (6 * D + 2 * TAIL_HIDDEN_CHUNK) * 4)),
        name="tail",
    )(x, oa, ob, sga, sgb, mod3, row(norm2_g), row(final_g),
      w_proj_a.astype(BF16), w_proj_b.astype(BF16), w_out.astype(BF16),
      w_ffn_in.astype(BF16), w_ffn_out.astype(BF16))


def kernel(x, c, w_ada, b_ada, norm1_g, norm2_g, w_in, lambda_q1, lambda_k1, lambda_q2, lambda_k2,
           subln_g, ln_v_g, ln_v_b, w_spatial, b_spatial, w_proj_a, w_proj_b, w_out, w_ffn_in,
           w_ffn_out, rel_bias, final_g):
    B, S, D = x.shape
    assert w_ada.shape[0] == 1, "single-layer block"
    assert S % INPROJ_TILE == 0 and S % TAIL_TILE == 0
    assert INPROJ_TILE % INPROJ_ROWS == 0 and INPROJ_ROWS % SEQ_TILE == 0
    assert INPROJ_ROWS % CHUNK == 0 and TAIL_TILE % TAIL_ROWS == 0
    assert SEQ_TILE >= REL_MAX_DIST, "bias must be constant beyond the first off-diagonal tile"

    mod3 = _ada(c, w_ada[0], b_ada[0]).reshape(B, 6, D)
    tbl = _bias_tiles(rel_bias, SEQ_TILE)
    qT, k, vT, ob, sga, sgb = _inproj(x, mod3, norm1_g[0], w_in[0], ln_v_g[0], ln_v_b[0],
                                      w_spatial[0], b_spatial[0])
    lam_vecs = jnp.concatenate([lambda_q1, lambda_k1, lambda_q2, lambda_k2], axis=0)
    oa = _attention(qT, k, vT, tbl, lam_vecs, subln_g[0])
    return _tail(x, oa, ob, sga, sgb, mod3, norm2_g[0], final_g, w_proj_a[0], w_proj_b[0],
                 w_out[0], w_ffn_in[0], w_ffn_out[0])
```

```python
import functools
import math

import numpy as np
import jax
import jax.numpy as jnp
from jax import lax
from jax.experimental import pallas as pl
from jax.experimental.pallas import tpu as pltpu

F32 = jnp.float32
BF16 = jnp.bfloat16

RMS_EPS = 1e-6
LN_EPS = 1e-5
HEAD_DIM = 64
V_DIM = 2 * HEAD_DIM
CHUNK = 128
GROUP_WIDTH = 128
REL_BUCKETS = 32
REL_MAX_EXACT = REL_BUCKETS // 2
REL_MAX_DIST = 128
LAM_INIT = 0.8 - 0.6 * math.exp(-0.3 * 0)
MASK_VALUE = -1e30
LOG2E = math.log2(math.e)
ONES_ROWS = 16
ATTN_HEADS_PER_STEP = 2
KEY_CHUNK = 256
EXP_LAG = 8
VALUE_LAG = 2

SEQ_TILE = 256
INPROJ_TILE = 512
INPROJ_ROWS = 512
INPROJ_SLAB = 256
INPROJ_EARLY_CHUNKS = 2
INPROJ_COLS = 256
TAIL_TILE = 512
TAIL_ROWS = 256
TAIL_HIDDEN_CHUNK = 1536
V7X_VMEM_BYTES = 64 * 1024 * 1024
VMEM_HEADROOM = 6 * 1024 * 1024
ATTN_TEMP_BYTES = 46 * 1024 * 1024

SEC_Q, SEC_K, SEC_V, SEC_U, SEC_G, SEC_GA, SEC_GB = range(7)


def _sigmoid(x):
    return 1.0 / (1.0 + jnp.exp2(x * (-LOG2E)))


def _gelu_tanh(x):
    a = -2.0 * math.sqrt(2.0 / math.pi) * LOG2E
    return x / (1.0 + jnp.exp2(x * (a + (a * 0.044715) * (x * x))))


def _vmem_limit(resident, streamed, temporaries):
    return int(min(resident + 2 * streamed + temporaries, V7X_VMEM_BYTES - VMEM_HEADROOM))


def _resident(shape):
    n = len(shape)
    return pl.BlockSpec(shape, lambda *_: (0,) * n, pipeline_mode=pl.Buffered(1))


def _ada_kernel(c_ref, w_ref, b_ref, o_ref):
    c = c_ref[...]
    ca = (c * _sigmoid(c)).astype(BF16)
    o_ref[...] = jnp.dot(ca, w_ref[...].astype(BF16), preferred_element_type=F32) + b_ref[...]


def _ada(c, w_ada, b_ada):
    B, D = c.shape
    N = w_ada.shape[1]
    tn = 1024
    return pl.pallas_call(
        _ada_kernel,
        grid=(N // tn,),
        in_specs=[pl.BlockSpec((B, D), lambda n: (0, 0)),
                  pl.BlockSpec((D, tn), lambda n: (0, n)),
                  pl.BlockSpec((1, tn), lambda n: (0, n))],
        out_specs=pl.BlockSpec((B, tn), lambda n: (0, n)),
        out_shape=jax.ShapeDtypeStruct((B, N), F32),
        name="ada",
    )(c, w_ada, b_ada.reshape(1, N))


def _bucket_tiles(T):
    kk = np.arange(T, dtype=np.int32)[:, None]
    qq = np.arange(T, dtype=np.int32)[None, :]
    out = []
    for o in (1, 0):
        dist = o * T + qq - kk
        n = np.maximum(dist, 0)
        nf = np.maximum(n, 1).astype(np.float32)
        large = REL_MAX_EXACT + (np.log(nf / np.float32(REL_MAX_EXACT))
                                 / np.float32(math.log(REL_MAX_DIST / REL_MAX_EXACT))
                                 * np.float32(REL_BUCKETS - REL_MAX_EXACT)).astype(np.int32)
        large = np.minimum(large, REL_BUCKETS - 1)
        b = np.where(n < REL_MAX_EXACT, n, large)
        out.append(np.where(dist < 0, -1, b).astype(np.int32))
    return np.concatenate(out, axis=0)


def _bias_kernel(rb_ref, bucket_ref, o_ref, *, layout):
    R = bucket_ref.shape[-1]
    for h in range(o_ref.shape[0]):
        far = rb_ref[REL_BUCKETS - 1, h]
        blocks = []
        for n in range(bucket_ref.shape[0]):
            bk = bucket_ref[n]
            acc = jnp.where(bk < 0, MASK_VALUE, 0.0).astype(F32)
            for b in range(REL_BUCKETS - 1):
                acc = jnp.where(bk == b, (rb_ref[b, h] - far) * LOG2E, acc)
            blocks.append(acc)
        for r, c, kind in layout:
            if kind == "far":
                blk = jnp.zeros((R, R), F32)
            elif kind == "masked":
                blk = jnp.full((R, R), MASK_VALUE, F32)
            else:
                blk = blocks[kind]
            o_ref[h, r:r + R, c:c + R] = blk


def _bias_tiles(rel_bias, T):
    H = rel_bias.shape[1]
    R = REL_MAX_DIST
    assert T % R == 0
    full = _bucket_tiles(T)
    distinct, index, layout = [], {}, []
    for r in range(0, 2 * T, R):
        for c in range(0, T, R):
            blk = full[r:r + R, c:c + R]
            if (blk == REL_BUCKETS - 1).all():
                kind = "far"
            elif (blk == -1).all():
                kind = "masked"
            else:
                kind = index.setdefault(blk.tobytes(), len(distinct))
                if kind == len(distinct):
                    distinct.append(blk)
            layout.append((r, c, kind))
    buckets = jnp.asarray(np.stack(distinct))
    return pl.pallas_call(
        functools.partial(_bias_kernel, layout=layout),
        grid=(1,),
        in_specs=[pl.BlockSpec(memory_space=pltpu.SMEM),
                  pl.BlockSpec(buckets.shape, lambda i: (0, 0, 0))],
        out_specs=pl.BlockSpec((H, 2 * T, T), lambda i: (0, 0, 0)),
        out_shape=jax.ShapeDtypeStruct((H, 2 * T, T), F32),
        name="bias_tiles",
    )(rel_bias, buckets)


def _inproj_kernel(x_ref, mod_ref, n1g_ref, w_ref, lng_ref, lnb_ref, ws_ref, bs_ref,
                   qT_ref, k_ref, vT_ref, ob_ref, sga_ref, sgb_ref):
    tm, D = x_ref.shape
    NC = INPROJ_COLS
    T = qT_ref.shape[-1]
    chunks = [slice(c * NC, (c + 1) * NC) for c in range(D // NC)]

    rnd = [0]
    queue = []

    def put(fn, *args):
        queue.append((rnd[0], functools.partial(fn, *args)))

    R = INPROJ_ROWS

    def row_group(j):
        rs = slice(j * R, (j + 1) * R)
        g_parts, u_parts, v_parts, stats = {}, {}, {}, {}

        def normed(rows):
            x = x_ref[rs.start + rows.start:rs.start + rows.stop, :]
            y = x * lax.rsqrt(jnp.mean(x * x, axis=-1, keepdims=True) + RMS_EPS) * n1g_ref[...]
            return (y * (1.0 + mod_ref[1:2, :]) + mod_ref[0:1, :]).astype(BF16)

        def epi_g(c, z):
            g = _gelu_tanh(z)
            g_parts[c] = g
            rsum = jnp.sum(g, axis=-1, keepdims=True)
            stats["sum"] = rsum if c == 0 else stats["sum"] + rsum
            if c == len(chunks) - 1:
                for cc in range(len(chunks)):
                    put(ln_center, cc)
                for cc in range(len(chunks)):
                    put(ln_scale, cc)

        def ln_center(c):
            gc = g_parts[c] - stats["sum"] * (1.0 / D)
            g_parts[c] = gc
            sq = jnp.sum(gc * gc, axis=-1, keepdims=True)
            stats["sq"] = sq if c == 0 else stats["sq"] + sq

        def ln_scale(c):
            rstd = lax.rsqrt(stats["sq"] * (1.0 / D) + LN_EPS)
            v = g_parts.pop(c) * rstd * lng_ref[:, chunks[c]] + lnb_ref[:, chunks[c]]
            v_parts[c] = v.astype(BF16)

        def epi_u(c, z):
            u_parts[c] = _gelu_tanh(z)

        def epi_k(c, z):
            k_ref[rs, chunks[c]] = z.astype(BF16)

        def epi_q(c, z):
            qz = (z.T * (LOG2E / math.sqrt(HEAD_DIM))).astype(BF16)
            for jj in range(R // T):
                qT_ref[j * (R // T) + jj, chunks[c], :] = qz[:, jj * T:(jj + 1) * T]

        def epi_v(c, z):
            vT_ref[chunks[c], rs] = z.T.astype(BF16)

        def epi_gate(ref, c, z):
            ref[rs, chunks[c]] = _sigmoid(z).astype(BF16)

        chunk_rows = [slice(ci * CHUNK, (ci + 1) * CHUNK) for ci in range(R // CHUNK)]

        def group_cols(gi):
            lo = gi * GROUP_WIDTH
            return lo // NC, slice(lo % NC, lo % NC + GROUP_WIDTH)

        def epi_sv(gi, sv):
            c, local = group_cols(gi)
            cols = slice(gi * GROUP_WIDTH, (gi + 1) * GROUP_WIDTH)
            for ci, rows in enumerate(chunk_rows):
                sv_c = sv[:, ci * GROUP_WIDTH:(ci + 1) * GROUP_WIDTH]
                out_rows = slice(j * R + rows.start, j * R + rows.stop)
                ob_ref[out_rows, cols] = (u_parts[c][rows, local] * sv_c).astype(BF16)

        def rows_dot(sec, cols, lhs=None):
            return jnp.dot(h if lhs is None else lhs,
                           w_ref[:, sec * D + cols.start:sec * D + cols.stop],
                           preferred_element_type=F32)

        heavy = ([(epi_g, (c,), rows_dot, (SEC_G, s)) for c, s in enumerate(chunks)]
                 + [(epi_u, (c,), rows_dot, (SEC_U, s)) for c, s in enumerate(chunks)]
                 + [(epi_gate, (sga_ref, c), rows_dot, (SEC_GA, s)) for c, s in enumerate(chunks)]
                 + [(epi_gate, (sgb_ref, c), rows_dot, (SEC_GB, s)) for c, s in enumerate(chunks)])
        light = ([(epi_k, (c,), rows_dot, (SEC_K, s)) for c, s in enumerate(chunks)]
                 + [(epi_q, (c,), rows_dot, (SEC_Q, s)) for c, s in enumerate(chunks)]
                 + [(epi_v, (c,), rows_dot, (SEC_V, s)) for c, s in enumerate(chunks)])
        order = []
        while heavy or light:
            order += heavy[:1] + light[:1]
            heavy, light = heavy[1:], light[1:]
        tri = (lax.broadcasted_iota(jnp.int32, (CHUNK, CHUNK), 0)
               >= lax.broadcasted_iota(jnp.int32, (CHUNK, CHUNK), 1))

        def spatial(gi):
            ws = jnp.where(tri, ws_ref[gi], 0.0).astype(BF16)
            c, local = group_cols(gi)
            vcat = jnp.concatenate([v_parts[c][rows, local] for rows in chunk_rows], axis=1)
            sv = jnp.dot(ws, vcat, preferred_element_type=F32) + bs_ref[:, gi:gi + 1]
            put(epi_sv, gi, sv)

        early, order = order[:INPROJ_EARLY_CHUNKS], order[INPROJ_EARLY_CHUNKS:]
        assert all(dot_fn is rows_dot for _, _, dot_fn, _ in early)
        h_slabs, z_slabs = [], [[] for _ in early]
        for lo in range(0, R, INPROJ_SLAB):
            h_slabs.append(normed(slice(lo, lo + INPROJ_SLAB)))
            for zs, (_, _, _, dot_args) in zip(z_slabs, early):
                zs.append(rows_dot(*dot_args, lhs=h_slabs[-1]))
        h = jnp.concatenate(h_slabs, axis=0)
        for zs, (epi, epi_args, _, _) in zip(z_slabs, early):
            put(epi, *epi_args, jnp.concatenate(zs, axis=0))
        yield True

        spatial_todo = list(range(ws_ref.shape[0]))
        for epi, epi_args, dot_fn, dot_args in order:
            put(epi, *epi_args, dot_fn(*dot_args))
            yield True
            if spatial_todo and len(v_parts) == len(chunks) and len(u_parts) == len(chunks):
                spatial(spatial_todo.pop(0))
                yield True
        for gi in spatial_todo:
            spatial(gi)
            yield True

    live = [row_group(j) for j in range(tm // R)]
    while live or queue:
        live = [g for g in live if next(g, None) is not None]
        rnd[0] += 1
        ready = sum(1 for r, _ in queue if r < rnd[0] - 1) if live else len(queue)
        for _ in range(min(ready, 2 * len(live) if live else ready)):
            queue.pop(0)[1]()


def _inproj(x, mod3, norm1_g, w_in, ln_v_g, ln_v_b, w_spatial, b_spatial):
    B, S, D = x.shape
    T = SEQ_TILE
    nt = S // T
    tm = INPROJ_TILE
    w_bf = w_in.astype(BF16)
    G = w_spatial.shape[0]
    row = lambda a: a.reshape(1, D)
    tok = pl.BlockSpec((None, tm, D), lambda b, t: (b, t, 0))
    q_featT = pl.BlockSpec((None, tm // T, D, T), lambda b, t: (b, t, 0, 0))
    v_featT = pl.BlockSpec((None, D, tm), lambda b, t: (b, 0, t))
    out_tok = jax.ShapeDtypeStruct((B, S, D), BF16)
    return pl.pallas_call(
        _inproj_kernel,
        grid=(B, S // tm),
        in_specs=[tok,
                  pl.BlockSpec((None, 6, D), lambda b, t: (b, 0, 0)),
                  _resident((1, D)),
                  _resident(w_bf.shape),
                  _resident((1, D)), _resident((1, D)),
                  _resident((G, CHUNK, CHUNK)), _resident((CHUNK, G))],
        out_specs=[q_featT, tok, v_featT, tok, tok, tok],
        out_shape=[jax.ShapeDtypeStruct((B, nt, D, T), BF16), out_tok,
                   jax.ShapeDtypeStruct((B, D, S), BF16), out_tok, out_tok, out_tok],
        compiler_params=pltpu.CompilerParams(
            dimension_semantics=("arbitrary", "arbitrary"),
            vmem_limit_bytes=_vmem_limit(resident=w_bf.size * 2,
                                         streamed=tm * D * 4 + 6 * tm * D * 2,
                                         temporaries=8 * tm * D * 4)),
        name="inproj",
    )(x, mod3, row(norm1_g), w_bf, row(ln_v_g), row(ln_v_b),
      w_spatial, b_spatial.T)


def _attn_kernel(qT_ref, k_ref, vT_ref, tbl_ref, lam_ref, sg_ref, o_ref, va_ref):
    nt, _, T = qT_ref.shape
    n_heads = va_ref.shape[0]
    feat = lax.broadcasted_iota(jnp.int32, (V_DIM, T), 0)
    head_cols = [slice(hl * V_DIM, (hl + 1) * V_DIM) for hl in range(n_heads)]
    for hl in range(n_heads):
        va_ref[hl, 0:V_DIM, :] = vT_ref[head_cols[hl], :]
        va_ref[hl, V_DIM:, :] = jnp.ones((ONES_ROWS, va_ref.shape[2]), BF16)
    lam = (jnp.exp(jnp.sum(lam_ref[0:1, :] * lam_ref[1:2, :], axis=-1, keepdims=True))
           - jnp.exp(jnp.sum(lam_ref[2:3, :] * lam_ref[3:4, :], axis=-1, keepdims=True))
           + LAM_INIT)

    units = [(hl, i, mi) for i in reversed(range(nt)) for hl in range(n_heads) for mi in range(2)]
    tick = [0]
    s_done = {}
    p_done = {}
    outs = {}

    def key_chunks(i):
        near_lo, hi = max(i - 1, 0) * T, (i + 1) * T
        cuts = sorted(set(range(near_lo, -1, -KEY_CHUNK)) | {0})
        far = [(lo, nxt, None) for lo, nxt in zip(cuts[:-1], cuts[1:])]
        return far + [(near_lo, hi, slice(2 * T - (hi - near_lo), 2 * T))]

    def scores():
        for u in units:
            hl, i, mi = u
            in_map = (feat < HEAD_DIM) if mi == 0 else (feat >= HEAD_DIM)
            qz = jnp.where(in_map, qT_ref[i, head_cols[hl], :].astype(F32), 0.0).astype(BF16)
            for c, (lo, hi, tbl_rows) in enumerate(key_chunks(i)):
                s = jnp.dot(k_ref[lo:hi, head_cols[hl]], qz, preferred_element_type=F32)
                if tbl_rows is not None:
                    s = s + tbl_ref[hl, tbl_rows, :]
                s_done[u, c] = (s, tick[0])
                yield True

    def probs():
        for u in units:
            for c in range(len(key_chunks(u[1]))):
                while (u, c) not in s_done or tick[0] < s_done[u, c][1] + EXP_LAG:
                    yield False
                s, _ = s_done.pop((u, c))
                mc = jnp.max(s, axis=0, keepdims=True)
                p_done[u, c] = (jnp.exp2((s - mc).astype(BF16)), mc, tick[0])
                yield True

    def values():
        for u in units:
            hl, i, mi = u
            parts = []
            for c, (lo, hi, _) in enumerate(key_chunks(i)):
                while (u, c) not in p_done or tick[0] < p_done[u, c][2] + VALUE_LAG:
                    yield False
                p, mc, _ = p_done.pop((u, c))
                parts.append((jnp.dot(va_ref[hl, :, lo:hi], p, preferred_element_type=F32), mc))
                yield True
            m = functools.reduce(jnp.maximum, [mc for _, mc in parts])
            acc = functools.reduce(lambda a, b: a + b, [d * jnp.exp2(mc - m) for d, mc in parts])
            outs[u] = acc[0:V_DIM, :] * (1.0 / acc[V_DIM:V_DIM + 1, :])
            if mi == 1:
                oT = outs.pop((hl, i, 0)) - lam * outs.pop((hl, i, 1))
                yT = oT * lax.rsqrt(jnp.mean(oT * oT, axis=0, keepdims=True) + RMS_EPS)
                y = yT.T * sg_ref[...] * (1.0 - LAM_INIT)
                o_ref[i * T:(i + 1) * T, head_cols[hl]] = y.astype(BF16)

    streams = [scores(), probs(), values()]
    while streams:
        streams = [g for g in streams if next(g, None) is not None]
        tick[0] += 1


def _attention(qT, k, vT, tbl, lam_vecs, subln_g):
    B, nt, D, T = qT.shape
    S = nt * T
    H = D // V_DIM
    hp = ATTN_HEADS_PER_STEP
    W = hp * V_DIM
    return pl.pallas_call(
        _attn_kernel,
        grid=(B, H // hp),
        in_specs=[pl.BlockSpec((None, nt, W, T), lambda b, h: (b, 0, h, 0)),
                  pl.BlockSpec((None, S, W), lambda b, h: (b, 0, h)),
                  pl.BlockSpec((None, W, S), lambda b, h: (b, h, 0)),
                  pl.BlockSpec((hp, 2 * T, T), lambda b, h: (h, 0, 0)),
                  pl.BlockSpec((4, HEAD_DIM), lambda b, h: (0, 0)),
                  pl.BlockSpec((1, V_DIM), lambda b, h: (0, 0))],
        out_specs=pl.BlockSpec((None, S, W), lambda b, h: (b, 0, h)),
        out_shape=jax.ShapeDtypeStruct((B, S, D), BF16),
        scratch_shapes=[pltpu.VMEM((hp, V_DIM + ONES_ROWS, S), BF16)],
        compiler_params=pltpu.CompilerParams(
            dimension_semantics=("arbitrary", "arbitrary"),
            vmem_limit_bytes=_vmem_limit(resident=hp * (V_DIM + ONES_ROWS) * S * 2,
                                         streamed=4 * S * W * 2 + hp * 2 * T * T * 4,
                                         temporaries=ATTN_TEMP_BYTES)),
        name="attn",
    )(qT, k, vT, tbl, lam_vecs, subln_g.reshape(1, V_DIM))


def _tail_kernel(x_ref, oa_ref, ob_ref, sga_ref, sgb_ref, mod_ref, n2g_ref, fg_ref,
                 wpa_ref, wpb_ref, wo_ref, wfi_ref, wfo_ref, o_ref):
    hidden = wfo_ref.shape[0]
    tm = x_ref.shape[0]
    bounds = list(range(0, hidden, TAIL_HIDDEN_CHUNK)) + [hidden]

    def row_group(rows):
        pa = jnp.dot(oa_ref[rows, :], wpa_ref[...], preferred_element_type=F32)
        pb = jnp.dot(ob_ref[rows, :], wpb_ref[...], preferred_element_type=F32)
        yield True
        merged = (sga_ref[rows, :].astype(F32) * pa + sgb_ref[rows, :].astype(F32) * pb)
        x1 = x_ref[rows, :] + mod_ref[2:3, :] * jnp.dot(merged.astype(BF16), wo_ref[...],
                                                        preferred_element_type=F32)
        yield True
        y = x1 * lax.rsqrt(jnp.mean(x1 * x1, axis=-1, keepdims=True) + RMS_EPS) * n2g_ref[...]
        h2 = (y * (1.0 + mod_ref[4:5, :]) + mod_ref[3:4, :]).astype(BF16)
        ffn = None
        for lo, hi in zip(bounds[:-1], bounds[1:]):
            gate = jnp.dot(h2, wfi_ref[:, lo:hi], preferred_element_type=F32)
            up = jnp.dot(h2, wfi_ref[:, hidden + lo:hidden + hi], preferred_element_type=F32)
            yield True
            act = (gate * _sigmoid(gate) * up).astype(BF16)
            part = jnp.dot(act, wfo_ref[lo:hi, :], preferred_element_type=F32)
            ffn = part if ffn is None else ffn + part
        yield True
        x2 = x1 + mod_ref[5:6, :] * ffn
        o_ref[rows, :] = (x2 * lax.rsqrt(jnp.mean(x2 * x2, axis=-1, keepdims=True) + RMS_EPS)
                          * fg_ref[...])

    live = [row_group(slice(r, r + TAIL_ROWS)) for r in range(0, tm, TAIL_ROWS)]
    while live:
        live = [g for g in live if next(g, None) is not None]


def _tail(x, oa, ob, sga, sgb, mod3, norm2_g, final_g, w_proj_a, w_proj_b, w_out, w_ffn_in,
          w_ffn_out):
    B, S, D = x.shape
    T = TAIL_TILE
    hidden = w_ffn_out.shape[0]
    row = lambda a: a.reshape(1, D)
    tok = pl.BlockSpec((None, T, D), lambda b, t: (b, t, 0))
    return pl.pallas_call(
        _tail_kernel,
        grid=(B, S // T),
        in_specs=[tok, tok, tok, tok, tok,
                  pl.BlockSpec((None, 6, D), lambda b, t: (b, 0, 0)),
                  _resident((1, D)), _resident((1, D)),
                  _resident((D, D)), _resident((D, D)), _resident((D, D)),
                  _resident((D, 2 * hidden)), _resident((hidden, D))],
        out_specs=tok,
        out_shape=jax.ShapeDtypeStruct((B, S, D), F32),
        compiler_params=pltpu.CompilerParams(
            dimension_semantics=("arbitrary", "arbitrary"),
            vmem_limit_bytes=_vmem_limit(
                resident=(3 * D * D + 3 * D * hidden) * 2,
                streamed=2 * T * D * 4 + 4 * T * D * 2,
                temporaries=(T // TAIL_ROWS) * TAIL_ROWS * (6 * D + 2 * TAIL_HIDDEN_CHUNK) * 4)),
        name="tail",
    )(x, oa, ob, sga, sgb, mod3, row(norm2_g), row(final_g),
      w_proj_a.astype(BF16), w_proj_b.astype(BF16), w_out.astype(BF16),
      w_ffn_in.astype(BF16), w_ffn_out.astype(BF16))


def kernel(x, c, w_ada, b_ada, norm1_g, norm2_g, w_in, lambda_q1, lambda_k1, lambda_q2, lambda_k2,
           subln_g, ln_v_g, ln_v_b, w_spatial, b_spatial, w_proj_a, w_proj_b, w_out, w_ffn_in,
           w_ffn_out, rel_bias, final_g):
    B, S, D = x.shape
    assert w_ada.shape[0] == 1, "single-layer block"
    assert S % INPROJ_TILE == 0 and S % TAIL_TILE == 0
    assert INPROJ_TILE % INPROJ_ROWS == 0 and INPROJ_ROWS % SEQ_TILE == 0
    assert INPROJ_ROWS % CHUNK == 0 and TAIL_TILE % TAIL_ROWS == 0
    assert SEQ_TILE >= REL_MAX_DIST, "bias must be constant beyond the first off-diagonal tile"

    mod3 = _ada(c, w_ada[0], b_ada[0]).reshape(B, 6, D)
    tbl = _bias_tiles(rel_bias, SEQ_TILE)
    qT, k, vT, ob, sga, sgb = _inproj(x, mod3, norm1_g[0], w_in[0], ln_v_g[0], ln_v_b[0],
                                      w_spatial[0], b_spatial[0])
    lam_vecs = jnp.concatenate([lambda_q1, lambda_k1, lambda_q2, lambda_k2], axis=0)
    oa = _attention(qT, k, vT, tbl, lam_vecs, subln_g[0])
    return _tail(x, oa, ob, sga, sgb, mod3, norm2_g[0], final_g, w_proj_a[0], w_proj_b[0],
                 w_out[0], w_ffn_in[0], w_ffn_out[0])
```
